```python
import jax, jax.numpy as jnp
from jax import lax
import numpy as np

D_MODEL = 1024
BATCH = 32
SEQ = 2048
DEPTH = 1

HEAD_DIM = 64
N_ATTN_HEADS = 12
D_ATTN = N_ATTN_HEADS * HEAD_DIM
DILATED_BRANCHES = ((128, 1), (512, 4), (2048, 16))
ATTN_BLOCK = 128
ROPE_THETA = 500000.0
ROPE_DIM = HEAD_DIM // 4
N_SSD_HEADS = 12
SSD_HEAD_DIM = 64
D_SSD = N_SSD_HEADS * SSD_HEAD_DIM
SSD_GROUPS = 4
SSD_HEADS_PER_GROUP = N_SSD_HEADS // SSD_GROUPS
SSD_STATE = 128
CONV_WIDTH = 4
SSD_CHUNK = 128
D_CONV = D_SSD + 2 * SSD_GROUPS * SSD_STATE
D_MIX = D_ATTN + D_SSD
D_IN_PROJ = 3 * D_ATTN + D_SSD + D_CONV + N_SSD_HEADS
D_FF = ((8 * D_MODEL // 3 + 255) // 256) * 256
ALPHA = (2.0 * DEPTH) ** 0.25
BETA = (8.0 * DEPTH) ** -0.25
LN_EPS = 1e-5
RMS_EPS = 1e-6

kernel_name = 'hybrid_ssd_dilated_attn_macaron_deepnorm'


def layer_norm(t, g, b):
    tf = t.astype(jnp.float32)
    mu = jnp.mean(tf, axis=-1, keepdims=True)
    var = jnp.mean(jnp.square(tf - mu), axis=-1, keepdims=True)
    return ((tf - mu) * lax.rsqrt(var + LN_EPS) * g + b).astype(t.dtype)


def rms_norm(t, w):
    tf = t.astype(jnp.float32)
    return (tf * lax.rsqrt(jnp.mean(tf * tf, axis=-1, keepdims=True) + RMS_EPS) * w).astype(t.dtype)


def swiglu(t, w_gate, w_up, w_down):
    return (jax.nn.silu(t @ w_gate) * (t @ w_up)) @ w_down


def rotary_tables(positions):
    inv_freq = ROPE_THETA ** (-jnp.arange(0, ROPE_DIM, 2, dtype=jnp.float32) / ROPE_DIM)
    ang = positions.astype(jnp.float32)[..., None] * inv_freq
    return jnp.cos(ang)[:, :, None, :], jnp.sin(ang)[:, :, None, :]


def partial_rope(t, cos, sin):
    half = ROPE_DIM // 2
    cos = cos.astype(t.dtype)
    sin = sin.astype(t.dtype)
    t1 = t[..., :half]
    t2 = t[..., half:ROPE_DIM]
    return jnp.concatenate([t1 * cos - t2 * sin, t2 * cos + t1 * sin, t[..., ROPE_DIM:]], axis=-1)


def banded_causal_attention(q, k, v, wr):
    n, L, h, dh = q.shape
    nblk = -(-L // ATTN_BLOCK)
    lp = nblk * ATTN_BLOCK
    qp = jnp.pad(q, ((0, 0), (0, lp - L), (0, 0), (0, 0)))
    kp = jnp.pad(k, ((0, 0), (wr, lp - L), (0, 0), (0, 0)))
    vp = jnp.pad(v, ((0, 0), (wr, lp - L), (0, 0), (0, 0)))
    scale = dh ** -0.5
    q_off = jnp.arange(ATTN_BLOCK)
    k_off = jnp.arange(ATTN_BLOCK + wr)
    rel = q_off[:, None] + wr - k_off[None, :]
    band = (rel >= 0) & (rel <= wr)

    def one_block(i):
        start = i * ATTN_BLOCK
        qb = lax.dynamic_slice_in_dim(qp, start, ATTN_BLOCK, axis=1)
        kb = lax.dynamic_slice_in_dim(kp, start, ATTN_BLOCK + wr, axis=1)
        vb = lax.dynamic_slice_in_dim(vp, start, ATTN_BLOCK + wr, axis=1)
        m_k = start - wr + k_off
        mask = band & (m_k >= 0)[None, :]
        s = jnp.einsum('nqhd,nkhd->nhqk', qb, kb).astype(jnp.float32) * scale
        s = jnp.where(mask, s, -jnp.inf)
        lse = jax.nn.logsumexp(s, axis=-1)
        p = jnp.exp(s - lse[..., None])
        o = jnp.einsum('nhqk,nkhd->nqhd', p, vb.astype(jnp.float32))
        return o, jnp.transpose(lse, (0, 2, 1))

    o, lse = lax.map(one_block, jnp.arange(nblk))
    o = jnp.transpose(o, (1, 0, 2, 3, 4)).reshape(n, lp, h, dh)[:, :L]
    lse = jnp.transpose(lse, (1, 0, 2, 3)).reshape(n, lp, h)[:, :L]
    return o, lse


def dilated_branch(q, k, v, window, dilation):
    b, s, h, dh = q.shape
    L = s // dilation

    def to_residue(t):
        return jnp.transpose(t.reshape(b, L, dilation, h, dh), (0, 2, 1, 3, 4)).reshape(b * dilation, L, h, dh)

    o, lse = banded_causal_attention(to_residue(q), to_residue(k), to_residue(v), window // dilation)
    o = jnp.transpose(o.reshape(b, dilation, L, h, dh), (0, 2, 1, 3, 4)).reshape(b, s, h, dh)
    lse = jnp.transpose(lse.reshape(b, dilation, L, h), (0, 2, 1, 3)).reshape(b, s, h)
    return o, lse


def dilated_attention_mixture(q, k, v):
    outs, lses = [], []
    for window, dilation in DILATED_BRANCHES:
        o, lse = dilated_branch(q, k, v, window, dilation)
        outs.append(o)
        lses.append(lse)
    w = jax.nn.softmax(jnp.stack(lses, axis=0), axis=0)
    return jnp.einsum('gbsh,gbshd->bshd', w, jnp.stack(outs, axis=0))


def causal_depthwise_conv(u, w, bias):
    c = u.shape[-1]
    out = lax.conv_general_dilated(u, w[:, None, :].astype(u.dtype), window_strides=(1,),
                                   padding=((CONV_WIDTH - 1, 0),),
                                   dimension_numbers=('NWC', 'WIO', 'NWC'),
                                   feature_group_count=c)
    return out + bias


def ssd_chunked(xdt, dA, Bm, Cm):
    b, s, g, j, p = xdt.shape
    n = Bm.shape[-1]
    nc = s // SSD_CHUNK
    cl = SSD_CHUNK
    xdt = xdt.astype(jnp.float32).reshape(b, nc, cl, g, j, p)
    Bc = Bm.astype(jnp.float32).reshape(b, nc, cl, g, n)
    Cc = Cm.astype(jnp.float32).reshape(b, nc, cl, g, n)
    a = jnp.transpose(dA.reshape(b, nc, cl, g, j), (0, 3, 4, 1, 2))
    a_cum = jnp.cumsum(a, axis=-1)
    tri = jnp.tril(jnp.ones((cl, cl), dtype=bool))
    seg = a_cum[..., :, None] - a_cum[..., None, :]
    Lmat = jnp.exp(jnp.where(tri, seg, -jnp.inf))
    CB = jnp.einsum('bclgn,bcsgn->bgcls', Cc, Bc)
    y_diag = jnp.einsum('bgcls,bgjcls,bcsgjp->bclgjp', CB, Lmat, xdt)
    decay_states = jnp.exp(a_cum[..., -1:] - a_cum)
    states = jnp.einsum('bclgn,bgjcl,bclgjp->bcgjpn', Bc, decay_states, xdt)
    chunk_decay = jnp.exp(a_cum[..., -1])

    def step(h, inp):
        st, dec = inp
        return dec[..., None, None] * h + st, h

    h0 = jnp.zeros((b, g, j, p, n), jnp.float32)
    _, prev = lax.scan(step, h0, (jnp.moveaxis(states, 1, 0), jnp.moveaxis(chunk_decay, -1, 0)))
    prev = jnp.moveaxis(prev, 0, 1)
    y_off = jnp.einsum('bclgn,bcgjpn,bgjcl->bclgjp', Cc, prev, jnp.exp(a_cum))
    return (y_diag + y_off).reshape(b, s, g, j, p)


def hybrid_mixer(h, cos, sin, w_in, conv_w, conv_b, dt_bias, a_log, d_skip, attn_norm_w, ssd_norm_w, w_out):
    b, s, _ = h.shape
    proj = h @ w_in
    cuts = [D_ATTN, 2 * D_ATTN, 3 * D_ATTN, 3 * D_ATTN + D_SSD, 3 * D_ATTN + D_SSD + D_CONV]
    q, k, v, z, xbc, dt = jnp.split(proj, cuts, axis=-1)
    q = partial_rope(q.reshape(b, s, N_ATTN_HEADS, HEAD_DIM), cos, sin)
    k = partial_rope(k.reshape(b, s, N_ATTN_HEADS, HEAD_DIM), cos, sin)
    v = v.reshape(b, s, N_ATTN_HEADS, HEAD_DIM)
    attn = dilated_attention_mixture(q, k, v).astype(h.dtype).reshape(b, s, D_ATTN)
    attn = rms_norm(attn, attn_norm_w)
    xbc = jax.nn.silu(causal_depthwise_conv(xbc, conv_w, conv_b))
    xs, Bm, Cm = jnp.split(xbc, [D_SSD, D_SSD + SSD_GROUPS * SSD_STATE], axis=-1)
    xs = xs.reshape(b, s, SSD_GROUPS, SSD_HEADS_PER_GROUP, SSD_HEAD_DIM)
    Bm = Bm.reshape(b, s, SSD_GROUPS, SSD_STATE)
    Cm = Cm.reshape(b, s, SSD_GROUPS, SSD_STATE)
    dt = jax.nn.softplus(dt.astype(jnp.float32) + dt_bias).reshape(b, s, SSD_GROUPS, SSD_HEADS_PER_GROUP)
    A = -jnp.exp(a_log.astype(jnp.float32)).reshape(SSD_GROUPS, SSD_HEADS_PER_GROUP)
    y = ssd_chunked(xs.astype(jnp.float32) * dt[..., None], dt * A, Bm, Cm)
    y = y + d_skip.reshape(SSD_GROUPS, SSD_HEADS_PER_GROUP)[..., None] * xs
    y = y.astype(h.dtype).reshape(b, s, D_SSD)
    y = rms_norm(y * jax.nn.silu(z), ssd_norm_w)
    return jnp.concatenate([attn, y], axis=-1) @ w_out


def setup_inputs(seed: int = 0) -> dict:
    key = jax.random.key(seed)
    ks = jax.random.split(key, 32)
    f32 = jnp.float32
    nrm = lambda k, shape, std: jax.random.normal(k, shape, f32) * std
    x = jax.random.normal(ks[0], (BATCH, SEQ, D_MODEL), f32)
    positions = (jnp.arange(SEQ, dtype=jnp.int32)[None, :]
                 + jax.random.randint(ks[1], (BATCH, 1), 0, 4096, dtype=jnp.int32))
    col_scale = jnp.concatenate([jnp.ones((2 * D_ATTN,), f32), jnp.full((D_ATTN,), BETA, f32),
                                 jnp.ones((D_SSD + D_CONV,), f32), jnp.full((N_SSD_HEADS,), 0.1, f32)])
    w_in = nrm(ks[2], (DEPTH, D_MODEL, D_IN_PROJ), D_MODEL ** -0.5) * col_scale
    conv_w = nrm(ks[3], (DEPTH, CONV_WIDTH, D_CONV), CONV_WIDTH ** -0.5)
    conv_b = nrm(ks[4], (DEPTH, D_CONV), 0.01)
    dt0 = jnp.exp(jax.random.uniform(ks[5], (DEPTH, N_SSD_HEADS), f32, np.log(1e-3), np.log(1e-1)))
    dt_bias = dt0 + jnp.log(-jnp.expm1(-dt0))
    a_log = jnp.log(jax.random.uniform(ks[6], (DEPTH, N_SSD_HEADS), f32, 1.0, 16.0))
    d_skip = 1.0 + nrm(ks[7], (DEPTH, N_SSD_HEADS), 0.01)
    attn_norm_w = 1.0 + nrm(ks[8], (DEPTH, D_ATTN), 0.01)
    ssd_norm_w = 1.0 + nrm(ks[9], (DEPTH, D_SSD), 0.01)
    w_out = nrm(ks[10], (DEPTH, D_MIX, D_MODEL), BETA * D_MIX ** -0.5)

    def ffn(k0, k1, k2):
        return (nrm(k0, (DEPTH, D_MODEL, D_FF), D_MODEL ** -0.5),
                nrm(k1, (DEPTH, D_MODEL, D_FF), BETA * D_MODEL ** -0.5),
                nrm(k2, (DEPTH, D_FF, D_MODEL), BETA * D_FF ** -0.5))

    ffn1_gate, ffn1_up, ffn1_down = ffn(ks[11], ks[12], ks[13])
    ffn2_gate, ffn2_up, ffn2_down = ffn(ks[14], ks[15], ks[16])
    gain = lambda k: 1.0 + nrm(k, (DEPTH, D_MODEL), 0.01)
    bias = lambda k: nrm(k, (DEPTH, D_MODEL), 0.01)
    return {'x': x, 'positions': positions,
            'ln1_g': gain(ks[17]), 'ln1_b': bias(ks[18]),
            'ffn1_gate': ffn1_gate, 'ffn1_up': ffn1_up, 'ffn1_down': ffn1_down,
            'w_in': w_in, 'conv_w': conv_w, 'conv_b': conv_b, 'dt_bias': dt_bias, 'a_log': a_log,
            'd_skip': d_skip, 'attn_norm_w': attn_norm_w, 'ssd_norm_w': ssd_norm_w, 'w_out': w_out,
            'ln2_g': gain(ks[19]), 'ln2_b': bias(ks[20]),
            'ffn2_gate': ffn2_gate, 'ffn2_up': ffn2_up, 'ffn2_down': ffn2_down,
            'ln3_g': gain(ks[21]), 'ln3_b': bias(ks[22])}


def reference(x, positions, ln1_g, ln1_b, ffn1_gate, ffn1_up, ffn1_down, w_in, conv_w, conv_b,
              dt_bias, a_log, d_skip, attn_norm_w, ssd_norm_w, w_out, ln2_g, ln2_b,
              ffn2_gate, ffn2_up, ffn2_down, ln3_g, ln3_b):
    cos, sin = rotary_tables(positions)
    h = x
    for l in range(DEPTH):
        h = layer_norm(ALPHA * h + 0.5 * swiglu(h, ffn1_gate[l], ffn1_up[l], ffn1_down[l]), ln1_g[l], ln1_b[l])
        mix = hybrid_mixer(h, cos, sin, w_in[l], conv_w[l], conv_b[l], dt_bias[l], a_log[l], d_skip[l],
                           attn_norm_w[l], ssd_norm_w[l], w_out[l])
        h = layer_norm(ALPHA * h + mix, ln2_g[l], ln2_b[l])
        h = layer_norm(ALPHA * h + 0.5 * swiglu(h, ffn2_gate[l], ffn2_up[l], ffn2_down[l]), ln3_g[l], ln3_b[l])
    return h
```

```python
import functools

import jax
import jax.numpy as jnp
from jax import lax
from jax.experimental import pallas as pl
from jax.experimental.pallas import tpu as pltpu

F32 = jnp.float32
BF16 = jnp.bfloat16

D_MODEL = 1024
HEAD_DIM = 64
N_ATTN_HEADS = 12
D_ATTN = N_ATTN_HEADS * HEAD_DIM
DILATED_BRANCHES = ((128, 1), (512, 4), (2048, 16))
ATTN_BLOCK = 128
ROPE_THETA = 500000.0
ROPE_DIM = HEAD_DIM // 4
N_SSD_HEADS = 12
SSD_HEAD_DIM = 64
D_SSD = N_SSD_HEADS * SSD_HEAD_DIM
SSD_GROUPS = 4
SSD_HEADS_PER_GROUP = N_SSD_HEADS // SSD_GROUPS
SSD_STATE = 128
CONV_WIDTH = 4
SSD_CHUNK = 128
D_BC = SSD_GROUPS * SSD_STATE
D_CONV = D_SSD + 2 * D_BC
D_FF = 2816
LN_EPS = 1e-5
RMS_EPS = 1e-6

LANES = 128
N_PAIRS = D_ATTN // LANES
DT_PAD = LANES
OFF_Q, OFF_K, OFF_V = 0, D_ATTN, 2 * D_ATTN
OFF_Z = 3 * D_ATTN
OFF_XBC = OFF_Z + D_SSD
OFF_DT = OFF_XBC + D_CONV
D_IN_PAD = OFF_DT + DT_PAD
NEG = -1e30

TM = 512
FF_CHUNK = 256
SSD_STEP = 512
VMEM_LIMIT = 56 * 1024 * 1024

assert all(w // d == ATTN_BLOCK for w, d in DILATED_BRANCHES)
DILATIONS = tuple(d for _, d in DILATED_BRANCHES)


def _resident(shape):
    return pl.BlockSpec(shape, lambda *_: (0,) * len(shape), pipeline_mode=pl.Buffered(1))


def _layer_norm(t, g, b):
    mu = jnp.mean(t, axis=-1, keepdims=True)
    c = t - mu
    var = jnp.mean(c * c, axis=-1, keepdims=True)
    return c * lax.rsqrt(var + LN_EPS) * g + b


def _swiglu(xb, wg_ref, wu_ref, wd_ref, act_ref):
    for c in range(D_FF // FF_CHUNK):
        sl = slice(c * FF_CHUNK, (c + 1) * FF_CHUNK)
        g = jnp.dot(xb, wg_ref[:, sl], preferred_element_type=F32)
        u = jnp.dot(xb, wu_ref[:, sl], preferred_element_type=F32)
        act_ref[:, sl] = (jax.nn.silu(g) * u).astype(BF16)
    return jnp.dot(act_ref[...], wd_ref[...], preferred_element_type=F32)


def _ffn_ln_kernel(alpha, x_ref, wg_ref, wu_ref, wd_ref, g_ref, b_ref, o_ref, act_ref):
    x = x_ref[...]
    f = _swiglu(x.astype(BF16), wg_ref, wu_ref, wd_ref, act_ref)
    o_ref[...] = _layer_norm(alpha * x + 0.5 * f, g_ref[...], b_ref[...])


def _ffn_ln(x, wg, wu, wd, g, b, alpha):
    n = x.shape[0]
    row = pl.BlockSpec((TM, D_MODEL), lambda i: (i, 0))
    return pl.pallas_call(
        functools.partial(_ffn_ln_kernel, alpha),
        grid=(n // TM,),
        in_specs=[row, _resident((D_MODEL, D_FF)), _resident((D_MODEL, D_FF)),
                  _resident((D_FF, D_MODEL)), _resident((1, D_MODEL)), _resident((1, D_MODEL))],
        out_specs=row,
        out_shape=jax.ShapeDtypeStruct((n, D_MODEL), F32),
        scratch_shapes=[pltpu.VMEM((TM, D_FF), BF16)],
        compiler_params=pltpu.CompilerParams(
            dimension_semantics=("parallel",), vmem_limit_bytes=VMEM_LIMIT),
        name="ffn_ln",
    )(x, wg, wu, wd, g, b)


def _in_proj_kernel(h_ref, pos_ref, invf_ref, w_ref, dtb_ref,
                    q1, q4, q16, k1, k4, k16, v1, v4, v16, z_ref, xbc_ref, dt_ref, stage_ref):
    tm = h_ref.shape[0]
    hb = h_ref[...].astype(BF16)

    ang = pos_ref[...].astype(F32) * invf_ref[...]
    cos, sin = jnp.cos(ang), jnp.sin(ang)
    lane = lax.broadcasted_iota(jnp.int32, (1, LANES), 1) % HEAD_DIM
    half = ROPE_DIM // 2
    sin_lo = jnp.where(lane < half, -sin, 0.0)
    sin_hi = jnp.where((lane >= half) & (lane < ROPE_DIM), sin, 0.0)

    def rope(t):
        return (t * cos + pltpu.roll(t, LANES - half, 1) * sin_lo
                + pltpu.roll(t, half, 1) * sin_hi)

    def emit(slot, off, scale, rotate, r1, r4, r16):
        t = jnp.dot(hb, w_ref[:, off:off + D_ATTN], preferred_element_type=F32)
        for p in range(N_PAIRS):
            tp = t[:, p * LANES:(p + 1) * LANES]
            if rotate:
                tp = rope(tp)
            if scale != 1.0:
                tp = tp * scale
            stage_ref[slot * N_PAIRS + p] = tp
            r1[p] = tp.astype(BF16)
        for d, ref in ((DILATIONS[1], r4), (DILATIONS[2], r16)):
            for p in range(N_PAIRS):
                for r in range(d):
                    ref[p, :, r * LANES:(r + 1) * LANES] = stage_ref[
                        slot * N_PAIRS + p, pl.ds(r, tm // d, stride=d), :].astype(BF16)

    emit(0, OFF_Q, HEAD_DIM ** -0.5, True, q1, q4, q16)
    emit(1, OFF_K, 1.0, True, k1, k4, k16)
    emit(2, OFF_V, 1.0, False, v1, v4, v16)

    z_ref[...] = jnp.dot(hb, w_ref[:, OFF_Z:OFF_Z + D_SSD], preferred_element_type=F32)
    xbc_ref[...] = jnp.dot(hb, w_ref[:, OFF_XBC:OFF_XBC + D_CONV], preferred_element_type=F32)
    dt = jnp.dot(hb, w_ref[:, OFF_DT:OFF_DT + DT_PAD], preferred_element_type=F32)
    dt_ref[...] = jax.nn.softplus(dt + dtb_ref[...])


def _in_proj(h, pos, invf, w, dtb, b, s):
    n = h.shape[0]
    nt = s // TM
    row = lambda width: pl.BlockSpec((TM, width), lambda i, j: (i * nt + j, 0))
    qkv_shapes, qkv_specs = [], []
    for _ in range(3):
        for d in DILATIONS:
            qkv_shapes.append(jax.ShapeDtypeStruct((b, N_PAIRS, s // d, d * LANES), BF16))
            qkv_specs.append(pl.BlockSpec((None, N_PAIRS, TM // d, d * LANES),
                                          lambda i, j: (i, 0, j, 0)))
    return pl.pallas_call(
        _in_proj_kernel,
        grid=(b, nt),
        in_specs=[row(D_MODEL), row(1), _resident((1, LANES)), _resident((D_MODEL, D_IN_PAD)),
                  _resident((1, DT_PAD))],
        out_specs=qkv_specs + [row(D_SSD), row(D_CONV), row(DT_PAD)],
        out_shape=qkv_shapes + [jax.ShapeDtypeStruct((n, D_SSD), F32),
                                jax.ShapeDtypeStruct((n, D_CONV), F32),
                                jax.ShapeDtypeStruct((n, DT_PAD), F32)],
        scratch_shapes=[pltpu.VMEM((3 * N_PAIRS, TM, LANES), F32)],
        compiler_params=pltpu.CompilerParams(
            dimension_semantics=("parallel", "parallel"), vmem_limit_bytes=VMEM_LIMIT),
        name="in_proj",
    )(h, pos, invf, w, dtb)


def _attn_kernel(q1, k1, v1, q4, k4, v4, q16, k16, v16, nw_ref, o_ref, osc, lsc, acc):
    hp = pl.program_id(1)
    s = q1.shape[0]
    blk = ATTN_BLOCK
    lo = lax.broadcasted_iota(jnp.int32, (blk, LANES), 1) < HEAD_DIM
    srow = lax.broadcasted_iota(jnp.int32, (2 * blk, LANES), 0)
    scol = lax.broadcasted_iota(jnp.int32, (2 * blk, LANES), 1)
    qmask = (srow < blk) == (scol < HEAD_DIM)
    mask_diag = scol <= srow % blk
    wrow = lax.broadcasted_iota(jnp.int32, (2 * blk, 2 * blk), 0) % blk
    wcol = lax.broadcasted_iota(jnp.int32, (2 * blk, 2 * blk), 1)
    mask_win = ((wcol < blk) & (wcol >= wrow)) | ((wcol >= blk) & (wcol - blk <= wrow))

    def block(q, kw, vw, mask):
        qs = jnp.concatenate([q, q], axis=0)
        qs = jnp.where(qmask, qs, jnp.zeros_like(qs))
        sc = lax.dot_general(qs, kw, (((1,), (1,)), ((), ())), preferred_element_type=F32)
        sc = jnp.where(mask, sc, NEG)
        m = jnp.max(sc, axis=-1, keepdims=True)
        p = jnp.exp(sc - m)
        l = jnp.sum(p, axis=-1, keepdims=True)
        o = jnp.dot(p.astype(BF16), vw, preferred_element_type=F32) / l
        lse = m + jnp.log(l)
        oc = jnp.where(lo, o[:blk], o[blk:])
        lc = jnp.where(lo, jnp.broadcast_to(lse[:blk], (blk, LANES)),
                       jnp.broadcast_to(lse[blk:], (blk, LANES)))
        return oc, lc

    for g, (d, qr, kr, vr) in enumerate(((DILATIONS[0], q1, k1, v1), (DILATIONS[1], q4, k4, v4),
                                         (DILATIONS[2], q16, k16, v16))):
        nblk = s // d // blk
        for r in range(d):
            ls = slice(r * LANES, (r + 1) * LANES)

            def put(i, oc, lc, d=d, r=r, g=g):
                if d == 1:
                    idx = pl.ds(pl.multiple_of(i * blk, blk), blk)
                else:
                    idx = pl.ds(i * blk * d + r, blk, stride=d)
                osc[g, idx, :] = oc
                lsc[g, idx, :] = lc

            oc, lc = block(qr[0:blk, ls], kr[0:blk, ls], vr[0:blk, ls], mask_diag)
            put(0, oc, lc)
            if nblk > 4:
                def body(i, carry, qr=qr, kr=kr, vr=vr, ls=ls, put=put):
                    q0 = pl.multiple_of(i * blk, blk)
                    k0 = pl.multiple_of((i - 1) * blk, blk)
                    oc, lc = block(qr[pl.ds(q0, blk), ls], kr[pl.ds(k0, 2 * blk), ls],
                                   vr[pl.ds(k0, 2 * blk), ls], mask_win)
                    put(i, oc, lc)
                    return carry
                lax.fori_loop(1, nblk, body, 0)
            else:
                for i in range(1, nblk):
                    oc, lc = block(qr[i * blk:(i + 1) * blk, ls], kr[(i - 1) * blk:(i + 1) * blk, ls],
                                   vr[(i - 1) * blk:(i + 1) * blk, ls], mask_win)
                    put(i, oc, lc)

    def merge(i, carry):
        rows = pl.ds(pl.multiple_of(i * blk, blk), blk)
        l0, l1, l2 = lsc[0, rows, :], lsc[1, rows, :], lsc[2, rows, :]
        m = jnp.maximum(jnp.maximum(l0, l1), l2)
        w0, w1, w2 = jnp.exp(l0 - m), jnp.exp(l1 - m), jnp.exp(l2 - m)
        num = w0 * osc[0, rows, :] + w1 * osc[1, rows, :] + w2 * osc[2, rows, :]
        acc[hp, rows, :] = num / (w0 + w1 + w2)
        return carry
    lax.fori_loop(0, s // blk, merge, 0)

    @pl.when(hp == N_PAIRS - 1)
    def _():
        def norm(i, carry):
            rows = pl.ds(pl.multiple_of(i * blk, blk), blk)
            ss = jnp.zeros((blk, 1), F32)
            for p in range(N_PAIRS):
                a = acc[p, rows, :]
                ss = ss + jnp.sum(a * a, axis=-1, keepdims=True)
            inv = lax.rsqrt(ss / D_ATTN + RMS_EPS)
            for p in range(N_PAIRS):
                ps = slice(p * LANES, (p + 1) * LANES)
                o_ref[rows, ps] = (acc[p, rows, :] * inv * nw_ref[:, ps]).astype(BF16)
            return carry
        lax.fori_loop(0, s // blk, norm, 0)


def _attention(qkv, nw, b, s):
    in_specs = []
    for _ in range(3):
        for d in DILATIONS:
            in_specs.append(pl.BlockSpec((None, None, s // d, d * LANES), lambda i, p: (i, p, 0, 0)))
    ops = [qkv[t * 3 + g] for g in range(3) for t in range(3)]
    specs = [in_specs[t * 3 + g] for g in range(3) for t in range(3)]
    return pl.pallas_call(
        _attn_kernel,
        grid=(b, N_PAIRS),
        in_specs=specs + [pl.BlockSpec((1, D_ATTN), lambda i, p: (0, 0))],
        out_specs=pl.BlockSpec((None, s, D_ATTN), lambda i, p: (i, 0, 0)),
        out_shape=jax.ShapeDtypeStruct((b, s, D_ATTN), BF16),
        scratch_shapes=[pltpu.VMEM((3, s, LANES), F32), pltpu.VMEM((3, s, LANES), F32),
                        pltpu.VMEM((N_PAIRS, s, LANES), F32)],
        compiler_params=pltpu.CompilerParams(
            dimension_semantics=("parallel", "arbitrary"), vmem_limit_bytes=VMEM_LIMIT),
        name="dilated_attention",
    )(*ops, nw)


def _ssd_kernel(xbc_ref, dt_ref, z_ref, cw_ref, cb_ref, alog_ref, dskip_ref, nw_ref, o_ref,
                ubuf, cbuf, hs, ybuf):
    t = xbc_ref.shape[0]
    cl = SSD_CHUNK
    halo = 8

    @pl.when(pl.program_id(1) == 0)
    def _():
        ubuf[0:halo, :] = jnp.zeros((halo, D_CONV), F32)
        hs[...] = jnp.zeros_like(hs)

    ubuf[halo:halo + t, :] = xbc_ref[...]
    conv = cb_ref[...] + sum(cw_ref[k:k + 1, :] * ubuf[halo - (CONV_WIDTH - 1) + k:
                                                      halo - (CONV_WIDTH - 1) + k + t, :]
                             for k in range(CONV_WIDTH))
    cbuf[...] = jax.nn.silu(conv)
    ubuf[0:halo, :] = ubuf[t:t + halo, :]

    lo = lax.broadcasted_iota(jnp.int32, (cl, LANES), 1) < SSD_HEAD_DIM
    ri = lax.broadcasted_iota(jnp.int32, (cl, cl), 0)
    ci = lax.broadcasted_iota(jnp.int32, (cl, cl), 1)
    tril = ri >= ci
    tril_f = tril.astype(F32)
    group_w = SSD_HEADS_PER_GROUP * SSD_HEAD_DIM
    gmask = (lax.broadcasted_iota(jnp.int32, (D_BC, D_SSD), 0) // SSD_STATE
             == lax.broadcasted_iota(jnp.int32, (D_BC, D_SSD), 1) // group_w)
    a_row = -jnp.exp(alog_ref[...])

    def expand(v):
        tiles = []
        for p in range(N_PAIRS):
            a = jnp.broadcast_to(v[:, 2 * p:2 * p + 1], (cl, LANES))
            b = jnp.broadcast_to(v[:, 2 * p + 1:2 * p + 2], (cl, LANES))
            tiles.append(jnp.where(lo, a, b))
        return jnp.concatenate(tiles, axis=1)

    for c in range(t // cl):
        rows = slice(c * cl, (c + 1) * cl)
        dt = dt_ref[rows, :]
        acum = jnp.dot(tril_f, dt * a_row, preferred_element_type=F32,
                       precision=lax.Precision.HIGHEST)
        acum_t = acum.T
        dt_x = expand(dt)
        acum_x = expand(acum)
        xs = cbuf[rows, 0:D_SSD]
        bm = cbuf[rows, D_SSD:D_SSD + D_BC].astype(BF16)
        cm = cbuf[rows, D_SSD + D_BC:D_CONV].astype(BF16)
        xdt = xs * dt_x
        last = acum_x[cl - 1:cl, :]
        xdecay = (xdt * jnp.exp(last - acum_x)).astype(BF16)
        xdt_b = xdt.astype(BF16)

        cbs = [lax.dot_general(cm[:, g * SSD_STATE:(g + 1) * SSD_STATE],
                               bm[:, g * SSD_STATE:(g + 1) * SSD_STATE],
                               (((1,), (1,)), ((), ())), preferred_element_type=F32)
               for g in range(SSD_GROUPS)]

        def gmat(hd):
            seg = acum[:, hd:hd + 1] - acum_t[hd:hd + 1, :]
            return (cbs[hd // SSD_HEADS_PER_GROUP]
                    * jnp.exp(jnp.where(tril, seg, NEG))).astype(BF16)

        for p in range(N_PAIRS):
            xp = xdt_b[:, p * LANES:(p + 1) * LANES]
            zero = jnp.zeros_like(xp)
            xpair = jnp.concatenate([jnp.where(lo, xp, zero), jnp.where(lo, zero, xp)], axis=0)
            gpair = jnp.concatenate([gmat(2 * p), gmat(2 * p + 1)], axis=1)
            ybuf[:, p * LANES:(p + 1) * LANES] = jnp.dot(gpair, xpair, preferred_element_type=F32)

        y_off = jnp.dot(cm, hs[...].astype(BF16), preferred_element_type=F32) * jnp.exp(acum_x)
        st = lax.dot_general(bm, xdecay, (((0,), (0,)), ((), ())), preferred_element_type=F32)
        hs[...] = hs[...] * jnp.exp(last) + jnp.where(gmask, st, 0.0)

        y = ybuf[...] + y_off + dskip_ref[...] * xs
        y = y * jax.nn.silu(z_ref[rows, :])
        inv = lax.rsqrt(jnp.mean(y * y, axis=-1, keepdims=True) + RMS_EPS)
        o_ref[rows, :] = (y * inv * nw_ref[...]).astype(BF16)


def _ssd(xbc, dt, z, cw, cb, alog, dskip, nw, b, s):
    n = xbc.shape[0]
    nt = s // SSD_STEP
    row = lambda width: pl.BlockSpec((SSD_STEP, width), lambda i, j: (i * nt + j, 0))
    const = lambda shape: pl.BlockSpec(shape, lambda i, j: (0, 0))
    return pl.pallas_call(
        _ssd_kernel,
        grid=(b, nt),
        in_specs=[row(D_CONV), row(DT_PAD), row(D_SSD), const((CONV_WIDTH, D_CONV)),
                  const((1, D_CONV)), const((1, DT_PAD)), const((1, D_SSD)), const((1, D_SSD))],
        out_specs=row(D_SSD),
        out_shape=jax.ShapeDtypeStruct((n, D_SSD), BF16),
        scratch_shapes=[pltpu.VMEM((SSD_STEP + 8, D_CONV), F32), pltpu.VMEM((SSD_STEP, D_CONV), F32),
                        pltpu.VMEM((D_BC, D_SSD), F32), pltpu.VMEM((SSD_CHUNK, D_SSD), F32)],
        compiler_params=pltpu.CompilerParams(
            dimension_semantics=("parallel", "arbitrary"), vmem_limit_bytes=VMEM_LIMIT),
        name="ssd",
    )(xbc, dt, z, cw, cb, alog, dskip, nw)


def _out_ffn_ln_kernel(alpha, a_ref, y_ref, h_ref, wo_ref, g2_ref, b2_ref,
                       wg_ref, wu_ref, wd_ref, g3_ref, b3_ref, o_ref, act_ref):
    mix = (jnp.dot(a_ref[...], wo_ref[0:D_ATTN, :], preferred_element_type=F32)
           + jnp.dot(y_ref[...], wo_ref[D_ATTN:D_ATTN + D_SSD, :], preferred_element_type=F32))
    h2 = _layer_norm(alpha * h_ref[...] + mix, g2_ref[...], b2_ref[...])
    f = _swiglu(h2.astype(BF16), wg_ref, wu_ref, wd_ref, act_ref)
    o_ref[...] = _layer_norm(alpha * h2 + 0.5 * f, g3_ref[...], b3_ref[...])


def _out_ffn_ln(attn, y, h, wo, g2, b2, wg, wu, wd, g3, b3, alpha):
    n = h.shape[0]
    row = lambda width: pl.BlockSpec((TM, width), lambda i: (i, 0))
    vec = _resident((1, D_MODEL))
    return pl.pallas_call(
        functools.partial(_out_ffn_ln_kernel, alpha),
        grid=(n // TM,),
        in_specs=[row(D_ATTN), row(D_SSD), row(D_MODEL), _resident((D_ATTN + D_SSD, D_MODEL)),
                  vec, vec, _resident((D_MODEL, D_FF)), _resident((D_MODEL, D_FF)),
                  _resident((D_FF, D_MODEL)), vec, vec],
        out_specs=row(D_MODEL),
        out_shape=jax.ShapeDtypeStruct((n, D_MODEL), F32),
        scratch_shapes=[pltpu.VMEM((TM, D_FF), BF16)],
        compiler_params=pltpu.CompilerParams(
            dimension_semantics=("parallel",), vmem_limit_bytes=VMEM_LIMIT),
        name="out_ffn_ln",
    )(attn, y, h, wo, g2, b2, wg, wu, wd, g3, b3)


def kernel(x, positions, ln1_g, ln1_b, ffn1_gate, ffn1_up, ffn1_down, w_in, conv_w, conv_b, dt_bias, a_log, d_skip, attn_norm_w, ssd_norm_w, w_out, ln2_g, ln2_b, ffn2_gate, ffn2_up, ffn2_down, ln3_g, ln3_b):
    b, s, d = x.shape
    depth = w_in.shape[0]
    assert d == D_MODEL and s % (TM * 1) == 0 and s % (DILATIONS[-1] * ATTN_BLOCK) == 0
    n = b * s
    alpha = (2.0 * depth) ** 0.25
    row = lambda v: v.reshape(1, -1).astype(F32)
    pad_lanes = lambda v: jnp.pad(row(v), ((0, 0), (0, LANES - v.shape[-1])))

    inv_freq = ROPE_THETA ** (-jnp.arange(0, ROPE_DIM, 2, dtype=F32) / ROPE_DIM)
    head_row = jnp.concatenate([inv_freq, inv_freq, jnp.zeros((HEAD_DIM - ROPE_DIM,), F32)])
    invf = jnp.tile(head_row, LANES // HEAD_DIM).reshape(1, LANES)

    h = x.reshape(n, d)
    pos = positions.reshape(n, 1)
    for l in range(depth):
        h = _ffn_ln(h, ffn1_gate[l].astype(BF16), ffn1_up[l].astype(BF16), ffn1_down[l].astype(BF16),
                    row(ln1_g[l]), row(ln1_b[l]), alpha)
        w = jnp.pad(w_in[l], ((0, 0), (0, D_IN_PAD - w_in.shape[-1]))).astype(BF16)
        *qkv, z, xbc, dt = _in_proj(h, pos, invf, w, pad_lanes(dt_bias[l]), b, s)
        attn = _attention(qkv, row(attn_norm_w[l]), b, s).reshape(n, D_ATTN)
        y = _ssd(xbc, dt, z, conv_w[l].astype(F32), row(conv_b[l]), pad_lanes(a_log[l]),
                 row(jnp.repeat(d_skip[l], SSD_HEAD_DIM)), row(ssd_norm_w[l]), b, s)
        h = _out_ffn_ln(attn, y, h, w_out[l].astype(BF16), row(ln2_g[l]), row(ln2_b[l]),
                        ffn2_gate[l].astype(BF16), ffn2_up[l].astype(BF16), ffn2_down[l].astype(BF16),
                        row(ln3_g[l]), row(ln3_b[l]), alpha)
    return h.reshape(b, s, d)
```

```python
import functools
import math

import jax
import jax.numpy as jnp
from jax import lax
from jax.experimental import pallas as pl
from jax.experimental.pallas import tpu as pltpu

F32 = jnp.float32
BF16 = jnp.bfloat16

D_MODEL = 1024
HEAD_DIM = 64
N_ATTN_HEADS = 12
D_ATTN = N_ATTN_HEADS * HEAD_DIM
DILATED_BRANCHES = ((128, 1), (512, 4), (2048, 16))
ATTN_BLOCK = 128
ROPE_THETA = 500000.0
ROPE_DIM = HEAD_DIM // 4
N_SSD_HEADS = 12
SSD_HEAD_DIM = 64
D_SSD = N_SSD_HEADS * SSD_HEAD_DIM
SSD_GROUPS = 4
SSD_HEADS_PER_GROUP = N_SSD_HEADS // SSD_GROUPS
SSD_STATE = 128
CONV_WIDTH = 4
SSD_CHUNK = 128
D_BC = SSD_GROUPS * SSD_STATE
D_CONV = D_SSD + 2 * D_BC
D_FF = 2816
LN_EPS = 1e-5
RMS_EPS = 1e-6

LANES = 128
SUBLANES = 8
N_PAIRS = D_ATTN // LANES
DT_PAD = LANES
OFF_Q, OFF_K, OFF_V = 0, D_ATTN, 2 * D_ATTN
OFF_Z = 3 * D_ATTN
OFF_XBC = OFF_Z + D_SSD
OFF_DT = OFF_XBC + D_CONV
D_IN_PAD = OFF_DT + DT_PAD
NEG = -1e30
LOG2E = math.log2(math.e)

TM = 512
FF_CHUNK = 256
SSD_STEP = 512
VMEM_LIMIT = 56 * 1024 * 1024

assert all(w // d == ATTN_BLOCK for w, d in DILATED_BRANCHES)
DILATIONS = tuple(d for _, d in DILATED_BRANCHES)
assert DILATIONS == (1, 4, 16)
STRIDE = DILATIONS[1]


def _resident(shape):
    return pl.BlockSpec(shape, lambda *_: (0,) * len(shape), pipeline_mode=pl.Buffered(1))


def _layer_norm(t, g, b):
    mu = jnp.mean(t, axis=-1, keepdims=True)
    c = t - mu
    var = jnp.mean(c * c, axis=-1, keepdims=True)
    return c * lax.rsqrt(var + LN_EPS) * g + b


def _swiglu(xb, wg_ref, wu_ref, wd_ref, act_ref):
    for c in range(D_FF // FF_CHUNK):
        sl = slice(c * FF_CHUNK, (c + 1) * FF_CHUNK)
        g = jnp.dot(xb, wg_ref[:, sl], preferred_element_type=F32)
        u = jnp.dot(xb, wu_ref[:, sl], preferred_element_type=F32)
        act_ref[:, sl] = (jax.nn.silu(g) * u).astype(BF16)
    return jnp.dot(act_ref[...], wd_ref[...], preferred_element_type=F32)


def _rope_kernel(pos_ref, invf_ref, cos_ref, sin_ref):
    ang = pos_ref[...].astype(F32) * invf_ref[...]
    cos_ref[...] = jnp.cos(ang)
    sin_ref[...] = jnp.sin(ang)


def _rope_tables(positions):
    n = positions.shape[0]
    half = ROPE_DIM // 2
    per_row = LANES // half
    rows = n // per_row
    inv_freq = ROPE_THETA ** (-jnp.arange(0, ROPE_DIM, 2, dtype=F32) / ROPE_DIM)
    pos_dense = jnp.repeat(positions, half).reshape(rows, LANES)
    invf = jnp.tile(inv_freq, per_row).reshape(1, LANES)
    blk = math.gcd(rows, 1024)
    spec = pl.BlockSpec((blk, LANES), lambda i: (i, 0))
    cos, sin = pl.pallas_call(
        _rope_kernel,
        grid=(rows // blk,),
        in_specs=[spec, pl.BlockSpec((1, LANES), lambda i: (0, 0))],
        out_specs=[spec, spec],
        out_shape=[jax.ShapeDtypeStruct((rows, LANES), F32)] * 2,
        name="rope_tables",
    )(pos_dense, invf)
    cos, sin = cos.reshape(n, half), sin.reshape(n, half)
    ones = jnp.ones((n, HEAD_DIM - ROPE_DIM), F32)
    zeros = jnp.zeros((n, HEAD_DIM - ROPE_DIM), F32)
    zh = jnp.zeros((n, half), F32)
    heads = LANES // HEAD_DIM
    cos_t = jnp.tile(jnp.concatenate([cos, cos, ones], axis=1), (1, heads))
    sin_lo = jnp.tile(jnp.concatenate([sin, zh, zeros], axis=1), (1, heads))
    sin_hi = jnp.tile(jnp.concatenate([zh, sin, zeros], axis=1), (1, heads))
    return cos_t, sin_lo, sin_hi


def _ffn_ln_kernel(alpha, x_ref, wg_ref, wu_ref, wd_ref, g_ref, b_ref, o_ref, act_ref):
    x = x_ref[...]
    f = _swiglu(x.astype(BF16), wg_ref, wu_ref, wd_ref, act_ref)
    o_ref[...] = _layer_norm(alpha * x + 0.5 * f, g_ref[...], b_ref[...])


def _ffn_ln(x, wg, wu, wd, g, b, alpha):
    n = x.shape[0]
    row = pl.BlockSpec((TM, D_MODEL), lambda i: (i, 0))
    return pl.pallas_call(
        functools.partial(_ffn_ln_kernel, alpha),
        grid=(n // TM,),
        in_specs=[row, _resident((D_MODEL, D_FF)), _resident((D_MODEL, D_FF)),
                  _resident((D_FF, D_MODEL)), _resident((1, D_MODEL)), _resident((1, D_MODEL))],
        out_specs=row,
        out_shape=jax.ShapeDtypeStruct((n, D_MODEL), F32),
        scratch_shapes=[pltpu.VMEM((TM, D_FF), BF16)],
        compiler_params=pltpu.CompilerParams(
            dimension_semantics=("parallel",), vmem_limit_bytes=VMEM_LIMIT),
        name="ffn_ln",
    )(x, wg, wu, wd, g, b)


def _in_proj_kernel(h_ref, cos_ref, sinlo_ref, sinhi_ref, w_ref, dtb_ref,
                    q1, q4, q16, k1, k4, k16, v1, v4, v16, z_ref, xbc_ref, dt_ref,
                    nat_ref, s4_ref):
    tm = h_ref.shape[0]
    hb = h_ref[...].astype(BF16)
    half = ROPE_DIM // 2
    q4n = tm // STRIDE
    q16n = q4n // STRIDE

    def project(slot, off, scale, rotate, r1):
        t = jnp.dot(hb, w_ref[:, off:off + D_ATTN], preferred_element_type=F32)
        for p in range(N_PAIRS):
            tp = t[:, p * LANES:(p + 1) * LANES]
            if rotate:
                tp = (tp * cos_ref[...] - pltpu.roll(tp, LANES - half, 1) * sinlo_ref[...]
                      + pltpu.roll(tp, half, 1) * sinhi_ref[...])
            if scale != 1.0:
                tp = tp * scale
            nat_ref[slot * N_PAIRS + p] = tp
            r1[p] = tp.astype(BF16)

    def reorder(slot, r4, r16):
        for p in range(N_PAIRS):
            i = slot * N_PAIRS + p
            for r in range(STRIDE):
                t4 = nat_ref[i, pl.ds(r, q4n, stride=STRIDE), :]
                s4_ref[i, r * q4n:(r + 1) * q4n, :] = t4
                r4[p, :, r * LANES:(r + 1) * LANES] = t4.astype(BF16)
            for r in range(STRIDE):
                for a in range(STRIDE):
                    r16c = a * STRIDE + r
                    r16[p, :, r16c * LANES:(r16c + 1) * LANES] = s4_ref[
                        i, pl.ds(r * q4n + a, q16n, stride=STRIDE), :].astype(BF16)

    project(0, OFF_Q, LOG2E * HEAD_DIM ** -0.5, True, q1)
    project(1, OFF_K, 1.0, True, k1)
    reorder(0, q4, q16)
    project(2, OFF_V, 1.0, False, v1)
    reorder(1, k4, k16)
    z_ref[...] = jnp.dot(hb, w_ref[:, OFF_Z:OFF_Z + D_SSD], preferred_element_type=F32)
    reorder(2, v4, v16)
    xbc_ref[...] = jnp.dot(hb, w_ref[:, OFF_XBC:OFF_XBC + D_CONV], preferred_element_type=F32)
    dt = jnp.dot(hb, w_ref[:, OFF_DT:OFF_DT + DT_PAD], preferred_element_type=F32)
    dt_ref[...] = jax.nn.softplus(dt + dtb_ref[...])


def _in_proj(h, tables, w, dtb, b, s):
    n = h.shape[0]
    nt = s // TM
    row = lambda width: pl.BlockSpec((TM, width), lambda i, j: (i * nt + j, 0))
    qkv_shapes, qkv_specs = [], []
    for _ in range(3):
        for d in DILATIONS:
            qkv_shapes.append(jax.ShapeDtypeStruct((b, N_PAIRS, s // d, d * LANES), BF16))
            qkv_specs.append(pl.BlockSpec((None, N_PAIRS, TM // d, d * LANES),
                                          lambda i, j: (i, 0, j, 0)))
    return pl.pallas_call(
        _in_proj_kernel,
        grid=(b, nt),
        in_specs=[row(D_MODEL), row(LANES), row(LANES), row(LANES),
                  _resident((D_MODEL, D_IN_PAD)), _resident((1, DT_PAD))],
        out_specs=qkv_specs + [row(D_SSD), row(D_CONV), row(DT_PAD)],
        out_shape=qkv_shapes + [jax.ShapeDtypeStruct((n, D_SSD), F32),
                                jax.ShapeDtypeStruct((n, D_CONV), F32),
                                jax.ShapeDtypeStruct((n, DT_PAD), F32)],
        scratch_shapes=[pltpu.VMEM((3 * N_PAIRS, TM, LANES), F32),
                        pltpu.VMEM((3 * N_PAIRS, TM, LANES), F32)],
        compiler_params=pltpu.CompilerParams(
            dimension_semantics=("parallel", "parallel"), vmem_limit_bytes=VMEM_LIMIT),
        name="in_proj",
    )(h, *tables, w, dtb)


def _attn_kernel(q1, k1, v1, q4, k4, v4, q16, k16, v16, nw_ref, o_ref,
                 va1, vb1, va4, vb4, va16, vb16, bias_ref, res1, res4, res16, acc):
    hp = pl.program_id(1)
    s = q1.shape[0]
    blk = ATTN_BLOCK
    sq = s // STRIDE
    lo = lax.broadcasted_iota(jnp.int32, (blk, LANES), 1) < HEAD_DIM

    wrow = lax.broadcasted_iota(jnp.int32, (2 * blk, 2 * blk), 0) % blk
    wcol = lax.broadcasted_iota(jnp.int32, (2 * blk, 2 * blk), 1)
    keep = ((wcol < blk) & (wcol >= wrow)) | ((wcol >= blk) & (wcol - blk <= wrow))
    bias_ref[...] = jnp.where(keep, 0.0, NEG)

    for v, va, vb in ((v1, va1, vb1), (v4, va4, vb4), (v16, va16, vb16)):
        first = lax.broadcasted_iota(jnp.int32, v.shape, 1) % LANES < HEAD_DIM
        vv = v[...]
        one = jnp.ones_like(vv)
        va[...] = jnp.where(first, vv, one)
        vb[...] = jnp.where(first, one, vv)

    def block(q, kw, vaw, vbw, bias):
        zero = jnp.zeros_like(q)
        qs = jnp.concatenate([jnp.where(lo, q, zero), jnp.where(lo, zero, q)], axis=0)
        sc = lax.dot_general(qs, kw, (((1,), (1,)), ((), ())), preferred_element_type=F32) + bias
        m = jnp.max(sc, axis=-1, keepdims=True)
        p = jnp.exp2(sc - m).astype(BF16)
        ra = jnp.dot(p[:blk], vaw, preferred_element_type=F32)
        rb = jnp.dot(p[blk:], vbw, preferred_element_type=F32)
        oc = jnp.where(lo, ra, rb)
        lx = jnp.where(lo, rb, ra)
        mc = jnp.where(lo, jnp.broadcast_to(m[:blk], (blk, LANES)),
                       jnp.broadcast_to(m[blk:], (blk, LANES)))
        return oc, mc, lx

    def run_branch(d, qr, kr, var, vbr, put):
        nblk = s // d // blk
        for r in range(d):
            ls = slice(r * LANES, (r + 1) * LANES)
            for i in range(nblk):
                k0 = max(i - 1, 0) * blk
                bias = bias_ref[...] if i > 0 else bias_ref[:, blk:]
                put(r, i, block(qr[i * blk:(i + 1) * blk, ls], kr[k0:(i + 1) * blk, ls],
                                var[k0:(i + 1) * blk, ls], vbr[k0:(i + 1) * blk, ls], bias))

    def put1(r, i, vals):
        for j, val in enumerate(vals):
            res1[j, i * blk:(i + 1) * blk, :] = val

    def put4(r, i, vals):
        for j, val in enumerate(vals):
            res4[j, r * sq + i * blk:r * sq + (i + 1) * blk, :] = val

    def put16(r, i, vals):
        a, r4 = divmod(r, STRIDE)
        for j, val in enumerate(vals):
            res16[j, pl.ds(r4 * sq + a, blk, stride=STRIDE), :] = val

    run_branch(DILATIONS[0], q1, k1, va1, vb1, put1)
    run_branch(DILATIONS[1], q4, k4, va4, vb4, put4)
    run_branch(DILATIONS[2], q16, k16, va16, vb16, put16)

    for r4 in range(STRIDE):
        for jb in range(sq // blk):
            nat = pl.ds(jb * STRIDE * blk + r4, blk, stride=STRIDE)
            rows = slice(r4 * sq + jb * blk, r4 * sq + (jb + 1) * blk)
            o = (res1[0, nat, :], res4[0, rows, :], res16[0, rows, :])
            m = (res1[1, nat, :], res4[1, rows, :], res16[1, rows, :])
            l = (res1[2, nat, :], res4[2, rows, :], res16[2, rows, :])
            mx = jnp.maximum(jnp.maximum(m[0], m[1]), m[2])
            e = [jnp.exp2(mi - mx) for mi in m]
            num = e[0] * o[0] + e[1] * o[1] + e[2] * o[2]
            den = sum(ei * pltpu.roll(li, HEAD_DIM, 1) for ei, li in zip(e, l))
            acc[hp, nat, :] = num / den

    @pl.when(hp == N_PAIRS - 1)
    def _():
        def norm(i, carry):
            rows = pl.ds(pl.multiple_of(i * blk, blk), blk)
            ss = jnp.zeros((blk, 1), F32)
            for p in range(N_PAIRS):
                a = acc[p, rows, :]
                ss = ss + jnp.sum(a * a, axis=-1, keepdims=True)
            inv = lax.rsqrt(ss / D_ATTN + RMS_EPS)
            for p in range(N_PAIRS):
                ps = slice(p * LANES, (p + 1) * LANES)
                o_ref[rows, ps] = (acc[p, rows, :] * inv * nw_ref[:, ps]).astype(BF16)
            return carry
        lax.fori_loop(0, s // blk, norm, 0)


def _attention(qkv, nw, b, s):
    in_specs = []
    for _ in range(3):
        for d in DILATIONS:
            in_specs.append(pl.BlockSpec((None, None, s // d, d * LANES), lambda i, p: (i, p, 0, 0)))
    ops = [qkv[t * 3 + g] for g in range(3) for t in range(3)]
    specs = [in_specs[t * 3 + g] for g in range(3) for t in range(3)]
    vscratch = []
    for d in DILATIONS:
        vscratch += [pltpu.VMEM((s // d, d * LANES), BF16)] * 2
    return pl.pallas_call(
        _attn_kernel,
        grid=(b, N_PAIRS),
        in_specs=specs + [pl.BlockSpec((1, D_ATTN), lambda i, p: (0, 0))],
        out_specs=pl.BlockSpec((None, s, D_ATTN), lambda i, p: (i, 0, 0)),
        out_shape=jax.ShapeDtypeStruct((b, s, D_ATTN), BF16),
        scratch_shapes=vscratch + [
            pltpu.VMEM((2 * ATTN_BLOCK, 2 * ATTN_BLOCK), F32),
            pltpu.VMEM((3, s, LANES), F32), pltpu.VMEM((3, s, LANES), F32),
            pltpu.VMEM((3, s, LANES), F32), pltpu.VMEM((N_PAIRS, s, LANES), F32)],
        compiler_params=pltpu.CompilerParams(
            dimension_semantics=("parallel", "arbitrary"), vmem_limit_bytes=VMEM_LIMIT),
        name="dilated_attention",
    )(*ops, nw)


def _ssd_kernel(xbc_ref, dt_ref, z_ref, cw_ref, cb_ref, alog_ref, dskip_ref, nw_ref, o_ref,
                ubuf, cbuf, hst, ybuf):
    t = xbc_ref.shape[0]
    cl = SSD_CHUNK
    halo = SUBLANES

    @pl.when(pl.program_id(1) == 0)
    def _():
        ubuf[0:halo, :] = jnp.zeros((halo, D_CONV), F32)
        hst[...] = jnp.zeros_like(hst)

    ubuf[halo:halo + t, :] = xbc_ref[...]
    u = ubuf[...]
    conv = cb_ref[...] + cw_ref[CONV_WIDTH - 1:CONV_WIDTH, :] * u[halo:halo + t]
    for k in range(1, CONV_WIDTH):
        conv = conv + (cw_ref[CONV_WIDTH - 1 - k:CONV_WIDTH - k, :]
                       * pltpu.roll(u, k, 0)[halo:halo + t])
    cbuf[...] = jax.nn.silu(conv)
    ubuf[0:halo, :] = ubuf[t:t + halo, :]

    lo = lax.broadcasted_iota(jnp.int32, (cl, LANES), 1) < SSD_HEAD_DIM
    ri = lax.broadcasted_iota(jnp.int32, (cl, cl), 0)
    ci = lax.broadcasted_iota(jnp.int32, (cl, cl), 1)
    tril = ri >= ci
    tril_f = tril.astype(F32)
    group_w = SSD_HEADS_PER_GROUP * SSD_HEAD_DIM
    a_row = -jnp.exp(alog_ref[...]) * LOG2E

    def expand(v):
        tiles = []
        for p in range(N_PAIRS):
            a = jnp.broadcast_to(v[:, 2 * p:2 * p + 1], (cl, LANES))
            b = jnp.broadcast_to(v[:, 2 * p + 1:2 * p + 2], (cl, LANES))
            tiles.append(jnp.where(lo, a, b))
        return jnp.concatenate(tiles, axis=1)

    def tile_groups(p):
        g0, g1 = (p * LANES) // group_w, ((p + 1) * LANES - 1) // group_w
        if g0 == g1:
            return (g0,), None
        first = lax.broadcasted_iota(jnp.int32, (cl, LANES), 1) < g1 * group_w - p * LANES
        return (g0, g1), first

    for c in range(t // cl):
        rows = slice(c * cl, (c + 1) * cl)
        dt = dt_ref[rows, :]
        acum = jnp.dot(tril_f, dt * a_row, preferred_element_type=F32,
                       precision=lax.Precision.HIGHEST)
        acum_t = acum.T
        dt_x = expand(dt)
        acum_x = expand(acum)
        xs = cbuf[rows, 0:D_SSD]
        bm = cbuf[rows, D_SSD:D_SSD + D_BC].astype(BF16)
        cm = cbuf[rows, D_SSD + D_BC:D_CONV].astype(BF16)
        bmg = [bm[:, g * SSD_STATE:(g + 1) * SSD_STATE] for g in range(SSD_GROUPS)]
        cmg = [cm[:, g * SSD_STATE:(g + 1) * SSD_STATE] for g in range(SSD_GROUPS)]
        xdt = xs * dt_x
        last = acum_x[cl - 1:cl, :]
        xdecay = (xdt * jnp.exp2(last - acum_x)).astype(BF16)
        xdt_b = xdt.astype(BF16)
        eacum = jnp.exp2(acum_x)
        cdecay = jnp.exp2(last)

        cbs = [lax.dot_general(cmg[g], bmg[g], (((1,), (1,)), ((), ())),
                               preferred_element_type=F32) for g in range(SSD_GROUPS)]

        def gmat(hd):
            seg = acum[:, hd:hd + 1] - acum_t[hd:hd + 1, :]
            return (cbs[hd // SSD_HEADS_PER_GROUP]
                    * jnp.exp2(jnp.where(tril, seg, NEG))).astype(BF16)

        for p in range(N_PAIRS):
            ps = slice(p * LANES, (p + 1) * LANES)
            xp = xdt_b[:, ps]
            zero = jnp.zeros_like(xp)
            xpair = jnp.concatenate([jnp.where(lo, xp, zero), jnp.where(lo, zero, xp)], axis=0)
            gpair = jnp.concatenate([gmat(2 * p), gmat(2 * p + 1)], axis=1)
            y = jnp.dot(gpair, xpair, preferred_element_type=F32)
            groups, first = tile_groups(p)
            hp_b = hst[:, ps].astype(BF16)
            offs = [jnp.dot(cmg[g], hp_b, preferred_element_type=F32) for g in groups]
            sts = [lax.dot_general(bmg[g], xdecay[:, ps], (((0,), (0,)), ((), ())),
                                   preferred_element_type=F32) for g in groups]
            off = offs[0] if first is None else jnp.where(first, offs[0], offs[1])
            st = sts[0] if first is None else jnp.where(first, sts[0], sts[1])
            ybuf[:, ps] = y + off * eacum[:, ps]
            hst[:, ps] = hst[:, ps] * cdecay[:, ps] + st

        y = ybuf[...] + dskip_ref[...] * xs
        y = y * jax.nn.silu(z_ref[rows, :])
        inv = lax.rsqrt(jnp.mean(y * y, axis=-1, keepdims=True) + RMS_EPS)
        o_ref[rows, :] = (y * inv * nw_ref[...]).astype(BF16)


def _ssd(xbc, dt, z, cw, cb, alog, dskip, nw, b, s):
    n = xbc.shape[0]
    nt = s // SSD_STEP
    row = lambda width: pl.BlockSpec((SSD_STEP, width), lambda i, j: (i * nt + j, 0))
    const = lambda shape: pl.BlockSpec(shape, lambda i, j: (0, 0))
    return pl.pallas_call(
        _ssd_kernel,
        grid=(b, nt),
        in_specs=[row(D_CONV), row(DT_PAD), row(D_SSD), const((CONV_WIDTH, D_CONV)),
                  const((1, D_CONV)), const((1, DT_PAD)), const((1, D_SSD)), const((1, D_SSD))],
        out_specs=row(D_SSD),
        out_shape=jax.ShapeDtypeStruct((n, D_SSD), BF16),
        scratch_shapes=[pltpu.VMEM((SSD_STEP + SUBLANES, D_CONV), F32),
                        pltpu.VMEM((SSD_STEP, D_CONV), F32),
                        pltpu.VMEM((SSD_STATE, D_SSD), F32), pltpu.VMEM((SSD_CHUNK, D_SSD), F32)],
        compiler_params=pltpu.CompilerParams(
            dimension_semantics=("parallel", "arbitrary"), vmem_limit_bytes=VMEM_LIMIT),
        name="ssd",
    )(xbc, dt, z, cw, cb, alog, dskip, nw)


def _out_ffn_ln_kernel(alpha, a_ref, y_ref, h_ref, wo_ref, g2_ref, b2_ref,
                       wg_ref, wu_ref, wd_ref, g3_ref, b3_ref, o_ref, act_ref):
    mix = (jnp.dot(a_ref[...], wo_ref[0:D_ATTN, :], preferred_element_type=F32)
           + jnp.dot(y_ref[...], wo_ref[D_ATTN:D_ATTN + D_SSD, :], preferred_element_type=F32))
    h2 = _layer_norm(alpha * h_ref[...] + mix, g2_ref[...], b2_ref[...])
    f = _swiglu(h2.astype(BF16), wg_ref, wu_ref, wd_ref, act_ref)
    o_ref[...] = _layer_norm(alpha * h2 + 0.5 * f, g3_ref[...], b3_ref[...])


def _out_ffn_ln(attn, y, h, wo, g2, b2, wg, wu, wd, g3, b3, alpha):
    n = h.shape[0]
    row = lambda width: pl.BlockSpec((TM, width), lambda i: (i, 0))
    vec = _resident((1, D_MODEL))
    return pl.pallas_call(
        functools.partial(_out_ffn_ln_kernel, alpha),
        grid=(n // TM,),
        in_specs=[row(D_ATTN), row(D_SSD), row(D_MODEL), _resident((D_ATTN + D_SSD, D_MODEL)),
                  vec, vec, _resident((D_MODEL, D_FF)), _resident((D_MODEL, D_FF)),
                  _resident((D_FF, D_MODEL)), vec, vec],
        out_specs=row(D_MODEL),
        out_shape=jax.ShapeDtypeStruct((n, D_MODEL), F32),
        scratch_shapes=[pltpu.VMEM((TM, D_FF), BF16)],
        compiler_params=pltpu.CompilerParams(
            dimension_semantics=("parallel",), vmem_limit_bytes=VMEM_LIMIT),
        name="out_ffn_ln",
    )(attn, y, h, wo, g2, b2, wg, wu, wd, g3, b3)


def kernel(x, positions, ln1_g, ln1_b, ffn1_gate, ffn1_up, ffn1_down, w_in, conv_w, conv_b, dt_bias, a_log, d_skip, attn_norm_w, ssd_norm_w, w_out, ln2_g, ln2_b, ffn2_gate, ffn2_up, ffn2_down, ln3_g, ln3_b):
    b, s, d = x.shape
    depth = w_in.shape[0]
    assert d == D_MODEL and s % TM == 0 and s % (DILATIONS[-1] * ATTN_BLOCK) == 0
    n = b * s
    alpha = (2.0 * depth) ** 0.25
    row = lambda v: v.reshape(1, -1).astype(F32)
    pad_lanes = lambda v: jnp.pad(row(v), ((0, 0), (0, LANES - v.shape[-1])))

    tables = _rope_tables(positions.reshape(n))
    h = x.reshape(n, d)
    for l in range(depth):
        h = _ffn_ln(h, ffn1_gate[l].astype(BF16), ffn1_up[l].astype(BF16), ffn1_down[l].astype(BF16),
                    row(ln1_g[l]), row(ln1_b[l]), alpha)
        w = jnp.pad(w_in[l], ((0, 0), (0, D_IN_PAD - w_in.shape[-1]))).astype(BF16)
        *qkv, z, xbc, dt = _in_proj(h, tables, w, pad_lanes(dt_bias[l]), b, s)
        attn = _attention(qkv, row(attn_norm_w[l]), b, s).reshape(n, D_ATTN)
        y = _ssd(xbc, dt, z, conv_w[l].astype(F32), row(conv_b[l]), pad_lanes(a_log[l]),
                 row(jnp.repeat(d_skip[l], SSD_HEAD_DIM)), row(ssd_norm_w[l]), b, s)
        h = _out_ffn_ln(attn, y, h, w_out[l].astype(BF16), row(ln2_g[l]), row(ln2_b[l]),
                        ffn2_gate[l].astype(BF16), ffn2_up[l].astype(BF16), ffn2_down[l].astype(BF16),
                        row(ln3_g[l]), row(ln3_b[l]), alpha)
    return h.reshape(b, s, d)
```

```python
import functools
import math

import jax
import jax.numpy as jnp
from jax import lax
from jax.experimental import pallas as pl
from jax.experimental.pallas import tpu as pltpu

F32 = jnp.float32
BF16 = jnp.bfloat16

D_MODEL = 1024
HEAD_DIM = 64
N_ATTN_HEADS = 12
D_ATTN = N_ATTN_HEADS * HEAD_DIM
DILATED_BRANCHES = ((128, 1), (512, 4), (2048, 16))
ATTN_BLOCK = 128
ROPE_THETA = 500000.0
ROPE_DIM = HEAD_DIM // 4
N_SSD_HEADS = 12
SSD_HEAD_DIM = 64
D_SSD = N_SSD_HEADS * SSD_HEAD_DIM
SSD_GROUPS = 4
SSD_HEADS_PER_GROUP = N_SSD_HEADS // SSD_GROUPS
SSD_STATE = 128
CONV_WIDTH = 4
SSD_CHUNK = 128
D_BC = SSD_GROUPS * SSD_STATE
D_CONV = D_SSD + 2 * D_BC
D_FF = 2816
LN_EPS = 1e-5
RMS_EPS = 1e-6

LANES = 128
SUBLANES = 8
N_PAIRS = D_ATTN // LANES
DT_PAD = LANES
OFF_Q, OFF_K, OFF_V = 0, D_ATTN, 2 * D_ATTN
OFF_Z = 3 * D_ATTN
OFF_XBC = OFF_Z + D_SSD
OFF_DT = OFF_XBC + D_CONV
D_IN_PAD = OFF_DT + DT_PAD
NEG = -1e30
LOG2E = math.log2(math.e)

TM = 512
FF_CHUNK = 256
SSD_STEP = 512
MERGE_UNROLL = 4
VMEM_LIMIT = 56 * 1024 * 1024

assert all(w // d == ATTN_BLOCK for w, d in DILATED_BRANCHES)
DILATIONS = tuple(d for _, d in DILATED_BRANCHES)
assert DILATIONS == (1, 4, 16)
STRIDE = DILATIONS[1]


def _resident(shape):
    return pl.BlockSpec(shape, lambda *_: (0,) * len(shape), pipeline_mode=pl.Buffered(1))


def _layer_norm(t, g, b):
    mu = jnp.mean(t, axis=-1, keepdims=True)
    c = t - mu
    var = jnp.mean(c * c, axis=-1, keepdims=True)
    return c * lax.rsqrt(var + LN_EPS) * g + b


def _swiglu(xb, wg_ref, wu_ref, wd_ref, act_ref):
    for c in range(D_FF // FF_CHUNK):
        sl = slice(c * FF_CHUNK, (c + 1) * FF_CHUNK)
        g = jnp.dot(xb, wg_ref[:, sl], preferred_element_type=F32)
        u = jnp.dot(xb, wu_ref[:, sl], preferred_element_type=F32)
        act_ref[:, sl] = (jax.nn.silu(g) * u).astype(BF16)
    return jnp.dot(act_ref[...], wd_ref[...], preferred_element_type=F32)


def _ffn_ln_kernel(alpha, x_ref, wg_ref, wu_ref, wd_ref, g_ref, b_ref, o_ref, act_ref):
    x = x_ref[...]
    f = _swiglu(x.astype(BF16), wg_ref, wu_ref, wd_ref, act_ref)
    o_ref[...] = _layer_norm(alpha * x + 0.5 * f, g_ref[...], b_ref[...])


def _ffn_ln(x, wg, wu, wd, g, b, alpha):
    n = x.shape[0]
    row = pl.BlockSpec((TM, D_MODEL), lambda i: (i, 0))
    return pl.pallas_call(
        functools.partial(_ffn_ln_kernel, alpha),
        grid=(n // TM,),
        in_specs=[row, _resident((D_MODEL, D_FF)), _resident((D_MODEL, D_FF)),
                  _resident((D_FF, D_MODEL)), _resident((1, D_MODEL)), _resident((1, D_MODEL))],
        out_specs=row,
        out_shape=jax.ShapeDtypeStruct((n, D_MODEL), F32),
        scratch_shapes=[pltpu.VMEM((TM, D_FF), BF16)],
        compiler_params=pltpu.CompilerParams(
            dimension_semantics=("parallel",), vmem_limit_bytes=VMEM_LIMIT),
        name="ffn_ln",
    )(x, wg, wu, wd, g, b)


def _in_proj_kernel(h_ref, pos_ref, freq_ref, w_ref, dtb_ref,
                    q1, q4, q16, k1, k4, k16, v1, v4, v16, z_ref, xbc_ref, dt_ref,
                    nat_ref, s4_ref):
    tm = h_ref.shape[0]
    hb = h_ref[...].astype(BF16)
    half = ROPE_DIM // 2
    q4n = tm // STRIDE
    q16n = q4n // STRIDE

    ang = freq_ref[...] * pos_ref[...].astype(F32)
    c8, s8 = jnp.cos(ang), jnp.sin(ang)
    z8 = jnp.zeros_like(c8)
    rest = HEAD_DIM - ROPE_DIM
    heads = LANES // HEAD_DIM
    cos_t = jnp.concatenate([c8, c8, jnp.ones((rest, tm), F32)] * heads, axis=0).T
    sin_lo = jnp.concatenate([s8, z8, jnp.zeros((rest, tm), F32)] * heads, axis=0).T
    sin_hi = jnp.concatenate([z8, s8, jnp.zeros((rest, tm), F32)] * heads, axis=0).T

    def project(slot, off, scale, rotate, r1):
        t = jnp.dot(hb, w_ref[:, off:off + D_ATTN], preferred_element_type=F32)
        for p in range(N_PAIRS):
            tp = t[:, p * LANES:(p + 1) * LANES]
            if rotate:
                tp = (tp * cos_t - pltpu.roll(tp, LANES - half, 1) * sin_lo
                      + pltpu.roll(tp, half, 1) * sin_hi)
            if scale != 1.0:
                tp = tp * scale
            nat_ref[slot * N_PAIRS + p] = tp
            r1[p] = tp.astype(BF16)

    def reorder(slot, r4, r16):
        for p in range(N_PAIRS):
            i = slot * N_PAIRS + p
            for r in range(STRIDE):
                t4 = nat_ref[i, pl.ds(r, q4n, stride=STRIDE), :]
                s4_ref[i, r * q4n:(r + 1) * q4n, :] = t4
                r4[p, :, r * LANES:(r + 1) * LANES] = t4.astype(BF16)
            for r in range(STRIDE):
                for a in range(STRIDE):
                    r16c = a * STRIDE + r
                    r16[p, :, r16c * LANES:(r16c + 1) * LANES] = s4_ref[
                        i, pl.ds(r * q4n + a, q16n, stride=STRIDE), :].astype(BF16)

    project(0, OFF_Q, LOG2E * HEAD_DIM ** -0.5, True, q1)
    project(1, OFF_K, 1.0, True, k1)
    reorder(0, q4, q16)
    project(2, OFF_V, 1.0, False, v1)
    reorder(1, k4, k16)
    z_ref[...] = jnp.dot(hb, w_ref[:, OFF_Z:OFF_Z + D_SSD], preferred_element_type=F32)
    reorder(2, v4, v16)
    xbc_ref[...] = jnp.dot(hb, w_ref[:, OFF_XBC:OFF_XBC + D_CONV], preferred_element_type=F32)
    dt = jnp.dot(hb, w_ref[:, OFF_DT:OFF_DT + DT_PAD], preferred_element_type=F32)
    dt_ref[...] = jax.nn.softplus(dt + dtb_ref[...])


def _in_proj(h, pos, freq, w, dtb, b, s):
    n = h.shape[0]
    nt = s // TM
    row = lambda width: pl.BlockSpec((TM, width), lambda i, j: (i * nt + j, 0))
    qkv_shapes, qkv_specs = [], []
    for _ in range(3):
        for d in DILATIONS:
            qkv_shapes.append(jax.ShapeDtypeStruct((b, N_PAIRS, s // d, d * LANES), BF16))
            qkv_specs.append(pl.BlockSpec((None, N_PAIRS, TM // d, d * LANES),
                                          lambda i, j: (i, 0, j, 0)))
    return pl.pallas_call(
        _in_proj_kernel,
        grid=(b, nt),
        in_specs=[row(D_MODEL), pl.BlockSpec((None, 1, TM), lambda i, j: (i * nt + j, 0, 0)),
                  _resident((ROPE_DIM // 2, 1)), _resident((D_MODEL, D_IN_PAD)),
                  _resident((1, DT_PAD))],
        out_specs=qkv_specs + [row(D_SSD), row(D_CONV), row(DT_PAD)],
        out_shape=qkv_shapes + [jax.ShapeDtypeStruct((n, D_SSD), F32),
                                jax.ShapeDtypeStruct((n, D_CONV), F32),
                                jax.ShapeDtypeStruct((n, DT_PAD), F32)],
        scratch_shapes=[pltpu.VMEM((3 * N_PAIRS, TM, LANES), F32),
                        pltpu.VMEM((3 * N_PAIRS, TM, LANES), F32)],
        compiler_params=pltpu.CompilerParams(
            dimension_semantics=("parallel", "parallel"), vmem_limit_bytes=VMEM_LIMIT),
        name="in_proj",
    )(h, pos, freq, w, dtb)


def _attn_kernel(q1, k1, v1, q4, k4, v4, q16, k16, v16, nw_ref, o_ref,
                 va1, vb1, va4, vb4, va16, vb16, bias_ref, res1, res4, res16, acc):
    hp = pl.program_id(1)
    s = q1.shape[0]
    blk = ATTN_BLOCK
    sq = s // STRIDE
    lo = lax.broadcasted_iota(jnp.int32, (blk, LANES), 1) < HEAD_DIM

    wrow = lax.broadcasted_iota(jnp.int32, (2 * blk, 2 * blk), 0) % blk
    wcol = lax.broadcasted_iota(jnp.int32, (2 * blk, 2 * blk), 1)
    keep = ((wcol < blk) & (wcol >= wrow)) | ((wcol >= blk) & (wcol - blk <= wrow))
    bias_ref[...] = jnp.where(keep, 0.0, NEG)

    for v, va, vb in ((v1, va1, vb1), (v4, va4, vb4), (v16, va16, vb16)):
        first = lax.broadcasted_iota(jnp.int32, v.shape, 1) % LANES < HEAD_DIM
        vv = v[...]
        one = jnp.ones_like(vv)
        va[...] = jnp.where(first, vv, one)
        vb[...] = jnp.where(first, one, vv)

    def block(q, kw, vaw, vbw, bias):
        zero = jnp.zeros_like(q)
        qs = jnp.concatenate([jnp.where(lo, q, zero), jnp.where(lo, zero, q)], axis=0)
        sc = lax.dot_general(qs, kw, (((1,), (1,)), ((), ())), preferred_element_type=F32) + bias
        m = jnp.max(sc, axis=-1, keepdims=True)
        p = jnp.exp2(sc - m).astype(BF16)
        ra = jnp.dot(p[:blk], vaw, preferred_element_type=F32)
        rb = jnp.dot(p[blk:], vbw, preferred_element_type=F32)
        oc = jnp.where(lo, ra, rb)
        lx = jnp.where(lo, rb, ra)
        mc = jnp.where(lo, jnp.broadcast_to(m[:blk], (blk, LANES)),
                       jnp.broadcast_to(m[blk:], (blk, LANES)))
        return oc, mc, lx

    def run_branch(d, qr, kr, var, vbr, put):
        nblk = s // d // blk
        for r in range(d):
            ls = slice(r * LANES, (r + 1) * LANES)
            for i in range(nblk):
                k0 = max(i - 1, 0) * blk
                bias = bias_ref[...] if i > 0 else bias_ref[:, blk:]
                put(r, i, block(qr[i * blk:(i + 1) * blk, ls], kr[k0:(i + 1) * blk, ls],
                                var[k0:(i + 1) * blk, ls], vbr[k0:(i + 1) * blk, ls], bias))

    def put1(r, i, vals):
        for j, val in enumerate(vals):
            res1[j, i * blk:(i + 1) * blk, :] = val

    def put4(r, i, vals):
        for j, val in enumerate(vals):
            res4[j, r * sq + i * blk:r * sq + (i + 1) * blk, :] = val

    def put16(r, i, vals):
        a, r4 = divmod(r, STRIDE)
        for j, val in enumerate(vals):
            res16[j, pl.ds(r4 * sq + a, blk, stride=STRIDE), :] = val

    run_branch(DILATIONS[0], q1, k1, va1, vb1, put1)
    run_branch(DILATIONS[1], q4, k4, va4, vb4, put4)
    run_branch(DILATIONS[2], q16, k16, va16, vb16, put16)

    nj = sq // blk

    def merge(it, carry):
        r4, jb = it // nj, it % nj
        nat = pl.ds(jb * (STRIDE * blk) + r4, blk, stride=STRIDE)
        rows = pl.ds(pl.multiple_of(r4 * sq + jb * blk, blk), blk)
        o = (res1[0, nat, :], res4[0, rows, :], res16[0, rows, :])
        m = (res1[1, nat, :], res4[1, rows, :], res16[1, rows, :])
        l = (res1[2, nat, :], res4[2, rows, :], res16[2, rows, :])
        mx = jnp.maximum(jnp.maximum(m[0], m[1]), m[2])
        e = [jnp.exp2(mi - mx) for mi in m]
        num = e[0] * o[0] + e[1] * o[1] + e[2] * o[2]
        den = sum(ei * pltpu.roll(li, HEAD_DIM, 1) for ei, li in zip(e, l))
        acc[hp, nat, :] = num / den
        return carry
    lax.fori_loop(0, STRIDE * nj, merge, 0, unroll=MERGE_UNROLL)

    @pl.when(hp == N_PAIRS - 1)
    def _():
        def norm(i, carry):
            rows = pl.ds(pl.multiple_of(i * blk, blk), blk)
            sq_sum = acc[0, rows, :] * acc[0, rows, :]
            for p in range(1, N_PAIRS):
                sq_sum = sq_sum + acc[p, rows, :] * acc[p, rows, :]
            inv = lax.rsqrt(jnp.sum(sq_sum, axis=-1, keepdims=True) / D_ATTN + RMS_EPS)
            for p in range(N_PAIRS):
                ps = slice(p * LANES, (p + 1) * LANES)
                o_ref[rows, ps] = (acc[p, rows, :] * inv * nw_ref[:, ps]).astype(BF16)
            return carry
        lax.fori_loop(0, s // blk, norm, 0, unroll=2)


def _attention(qkv, nw, b, s):
    in_specs = []
    for _ in range(3):
        for d in DILATIONS:
            in_specs.append(pl.BlockSpec((None, None, s // d, d * LANES), lambda i, p: (i, p, 0, 0)))
    ops = [qkv[t * 3 + g] for g in range(3) for t in range(3)]
    specs = [in_specs[t * 3 + g] for g in range(3) for t in range(3)]
    vscratch = []
    for d in DILATIONS:
        vscratch += [pltpu.VMEM((s // d, d * LANES), BF16)] * 2
    return pl.pallas_call(
        _attn_kernel,
        grid=(b, N_PAIRS),
        in_specs=specs + [pl.BlockSpec((1, D_ATTN), lambda i, p: (0, 0))],
        out_specs=pl.BlockSpec((None, s, D_ATTN), lambda i, p: (i, 0, 0)),
        out_shape=jax.ShapeDtypeStruct((b, s, D_ATTN), BF16),
        scratch_shapes=vscratch + [
            pltpu.VMEM((2 * ATTN_BLOCK, 2 * ATTN_BLOCK), F32),
            pltpu.VMEM((3, s, LANES), F32), pltpu.VMEM((3, s, LANES), F32),
            pltpu.VMEM((3, s, LANES), F32), pltpu.VMEM((N_PAIRS, s, LANES), F32)],
        compiler_params=pltpu.CompilerParams(
            dimension_semantics=("parallel", "arbitrary"), vmem_limit_bytes=VMEM_LIMIT),
        name="dilated_attention",
    )(*ops, nw)


def _ssd_kernel(xbc_ref, dt_ref, z_ref, cw_ref, cb_ref, alog_ref, dskip_ref, nw_ref, o_ref,
                ubuf, cbuf, hst, ybuf):
    t = xbc_ref.shape[0]
    cl = SSD_CHUNK
    halo = SUBLANES

    @pl.when(pl.program_id(1) == 0)
    def _():
        ubuf[0:halo, :] = jnp.zeros((halo, D_CONV), F32)
        hst[...] = jnp.zeros_like(hst)

    ubuf[halo:halo + t, :] = xbc_ref[...]
    assert CONV_WIDTH == 4
    u = ubuf[...]
    u1 = pltpu.roll(u, 1, 0)
    v2 = pltpu.roll(cw_ref[1:2, :] * u + cw_ref[0:1, :] * u1, 2, 0)
    conv = (cb_ref[...] + cw_ref[3:4, :] * u[halo:halo + t] + cw_ref[2:3, :] * u1[halo:halo + t]
            + v2[halo:halo + t])
    cbuf[...] = jax.nn.silu(conv)
    ubuf[0:halo, :] = ubuf[t:t + halo, :]

    lo = lax.broadcasted_iota(jnp.int32, (cl, LANES), 1) < SSD_HEAD_DIM
    ri = lax.broadcasted_iota(jnp.int32, (cl, cl), 0)
    ci = lax.broadcasted_iota(jnp.int32, (cl, cl), 1)
    tril = ri >= ci
    tril_f = tril.astype(F32)
    group_w = SSD_HEADS_PER_GROUP * SSD_HEAD_DIM
    a_row = -jnp.exp(alog_ref[...]) * LOG2E

    def expand(v):
        tiles = []
        for p in range(N_PAIRS):
            a = jnp.broadcast_to(v[:, 2 * p:2 * p + 1], (cl, LANES))
            b = jnp.broadcast_to(v[:, 2 * p + 1:2 * p + 2], (cl, LANES))
            tiles.append(jnp.where(lo, a, b))
        return jnp.concatenate(tiles, axis=1)

    def tile_groups(p):
        g0, g1 = (p * LANES) // group_w, ((p + 1) * LANES - 1) // group_w
        if g0 == g1:
            return (g0,), None
        first = lax.broadcasted_iota(jnp.int32, (cl, LANES), 1) < g1 * group_w - p * LANES
        return (g0, g1), first

    for c in range(t // cl):
        rows = slice(c * cl, (c + 1) * cl)
        dt = dt_ref[rows, :]
        acum = jnp.dot(tril_f, dt * a_row, preferred_element_type=F32,
                       precision=lax.Precision.HIGHEST)
        acum_t = acum.T
        dt_x = expand(dt)
        acum_x = expand(acum)
        xs = cbuf[rows, 0:D_SSD]
        bm = cbuf[rows, D_SSD:D_SSD + D_BC].astype(BF16)
        cm = cbuf[rows, D_SSD + D_BC:D_CONV].astype(BF16)
        bmg = [bm[:, g * SSD_STATE:(g + 1) * SSD_STATE] for g in range(SSD_GROUPS)]
        cmg = [cm[:, g * SSD_STATE:(g + 1) * SSD_STATE] for g in range(SSD_GROUPS)]
        xdt = xs * dt_x
        last = acum_x[cl - 1:cl, :]
        xdecay = (xdt * jnp.exp2(last - acum_x)).astype(BF16)
        xdt_b = xdt.astype(BF16)
        eacum = jnp.exp2(acum_x)
        cdecay = jnp.exp2(last)

        cbs = [lax.dot_general(cmg[g], bmg[g], (((1,), (1,)), ((), ())),
                               preferred_element_type=F32) for g in range(SSD_GROUPS)]

        def gmat(hd):
            seg = acum[:, hd:hd + 1] - acum_t[hd:hd + 1, :]
            return (cbs[hd // SSD_HEADS_PER_GROUP]
                    * jnp.exp2(jnp.where(tril, seg, NEG))).astype(BF16)

        for p in range(N_PAIRS):
            ps = slice(p * LANES, (p + 1) * LANES)
            xp = xdt_b[:, ps]
            zero = jnp.zeros_like(xp)
            xpair = jnp.concatenate([jnp.where(lo, xp, zero), jnp.where(lo, zero, xp)], axis=0)
            gpair = jnp.concatenate([gmat(2 * p), gmat(2 * p + 1)], axis=1)
            y = jnp.dot(gpair, xpair, preferred_element_type=F32)
            groups, first = tile_groups(p)
            hp_b = hst[:, ps].astype(BF16)
            offs = [jnp.dot(cmg[g], hp_b, preferred_element_type=F32) for g in groups]
            sts = [lax.dot_general(bmg[g], xdecay[:, ps], (((0,), (0,)), ((), ())),
                                   preferred_element_type=F32) for g in groups]
            off = offs[0] if first is None else jnp.where(first, offs[0], offs[1])
            st = sts[0] if first is None else jnp.where(first, sts[0], sts[1])
            ybuf[:, ps] = y + off * eacum[:, ps]
            hst[:, ps] = hst[:, ps] * cdecay[:, ps] + st

        y = ybuf[...] + dskip_ref[...] * xs
        y = y * jax.nn.silu(z_ref[rows, :])
        inv = lax.rsqrt(jnp.mean(y * y, axis=-1, keepdims=True) + RMS_EPS)
        o_ref[rows, :] = (y * inv * nw_ref[...]).astype(BF16)


def _ssd(xbc, dt, z, cw, cb, alog, dskip, nw, b, s):
    n = xbc.shape[0]
    nt = s // SSD_STEP
    row = lambda width: pl.BlockSpec((SSD_STEP, width), lambda i, j: (i * nt + j, 0))
    const = lambda shape: pl.BlockSpec(shape, lambda i, j: (0, 0))
    return pl.pallas_call(
        _ssd_kernel,
        grid=(b, nt),
        in_specs=[row(D_CONV), row(DT_PAD), row(D_SSD), const((CONV_WIDTH, D_CONV)),
                  const((1, D_CONV)), const((1, DT_PAD)), const((1, D_SSD)), const((1, D_SSD))],
        out_specs=row(D_SSD),
        out_shape=jax.ShapeDtypeStruct((n, D_SSD), BF16),
        scratch_shapes=[pltpu.VMEM((SSD_STEP + SUBLANES, D_CONV), F32),
                        pltpu.VMEM((SSD_STEP, D_CONV), F32),
                        pltpu.VMEM((SSD_STATE, D_SSD), F32), pltpu.VMEM((SSD_CHUNK, D_SSD), F32)],
        compiler_params=pltpu.CompilerParams(
            dimension_semantics=("parallel", "arbitrary"), vmem_limit_bytes=VMEM_LIMIT),
        name="ssd",
    )(xbc, dt, z, cw, cb, alog, dskip, nw)


def _out_ffn_ln_kernel(alpha, a_ref, y_ref, h_ref, wo_ref, g2_ref, b2_ref,
                       wg_ref, wu_ref, wd_ref, g3_ref, b3_ref, o_ref, act_ref):
    mix = (jnp.dot(a_ref[...], wo_ref[0:D_ATTN, :], preferred_element_type=F32)
           + jnp.dot(y_ref[...], wo_ref[D_ATTN:D_ATTN + D_SSD, :], preferred_element_type=F32))
    h2 = _layer_norm(alpha * h_ref[...] + mix, g2_ref[...], b2_ref[...])
    f = _swiglu(h2.astype(BF16), wg_ref, wu_ref, wd_ref, act_ref)
    o_ref[...] = _layer_norm(alpha * h2 + 0.5 * f, g3_ref[...], b3_ref[...])


def _out_ffn_ln(attn, y, h, wo, g2, b2, wg, wu, wd, g3, b3, alpha):
    n = h.shape[0]
    row = lambda width: pl.BlockSpec((TM, width), lambda i: (i, 0))
    vec = _resident((1, D_MODEL))
    return pl.pallas_call(
        functools.partial(_out_ffn_ln_kernel, alpha),
        grid=(n // TM,),
        in_specs=[row(D_ATTN), row(D_SSD), row(D_MODEL), _resident((D_ATTN + D_SSD, D_MODEL)),
                  vec, vec, _resident((D_MODEL, D_FF)), _resident((D_MODEL, D_FF)),
                  _resident((D_FF, D_MODEL)), vec, vec],
        out_specs=row(D_MODEL),
        out_shape=jax.ShapeDtypeStruct((n, D_MODEL), F32),
        scratch_shapes=[pltpu.VMEM((TM, D_FF), BF16)],
        compiler_params=pltpu.CompilerParams(
            dimension_semantics=("parallel",), vmem_limit_bytes=VMEM_LIMIT),
        name="out_ffn_ln",
    )(attn, y, h, wo, g2, b2, wg, wu, wd, g3, b3)


def kernel(x, positions, ln1_g, ln1_b, ffn1_gate, ffn1_up, ffn1_down, w_in, conv_w, conv_b, dt_bias, a_log, d_skip, attn_norm_w, ssd_norm_w, w_out, ln2_g, ln2_b, ffn2_gate, ffn2_up, ffn2_down, ln3_g, ln3_b):
    b, s, d = x.shape
    depth = w_in.shape[0]
    assert d == D_MODEL and s % TM == 0 and s % (DILATIONS[-1] * ATTN_BLOCK) == 0
    n = b * s
    alpha = (2.0 * depth) ** 0.25
    row = lambda v: v.reshape(1, -1).astype(F32)
    pad_lanes = lambda v: jnp.pad(row(v), ((0, 0), (0, LANES - v.shape[-1])))

    inv_freq = ROPE_THETA ** (-jnp.arange(0, ROPE_DIM, 2, dtype=F32) / ROPE_DIM)
    freq = inv_freq.reshape(ROPE_DIM // 2, 1)
    pos = positions.reshape(n // TM, 1, TM)
    h = x.reshape(n, d)
    for l in range(depth):
        h = _ffn_ln(h, ffn1_gate[l].astype(BF16), ffn1_up[l].astype(BF16), ffn1_down[l].astype(BF16),
                    row(ln1_g[l]), row(ln1_b[l]), alpha)
        w = jnp.pad(w_in[l], ((0, 0), (0, D_IN_PAD - w_in.shape[-1]))).astype(BF16)
        *qkv, z, xbc, dt = _in_proj(h, pos, freq, w, pad_lanes(dt_bias[l]), b, s)
        attn = _attention(qkv, row(attn_norm_w[l]), b, s).reshape(n, D_ATTN)
        y = _ssd(xbc, dt, z, conv_w[l].astype(F32), row(conv_b[l]), pad_lanes(a_log[l]),
                 row(jnp.repeat(d_skip[l], SSD_HEAD_DIM)), row(ssd_norm_w[l]), b, s)
        h = _out_ffn_ln(attn, y, h, w_out[l].astype(BF16), row(ln2_g[l]), row(ln2_b[l]),
                        ffn2_gate[l].astype(BF16), ffn2_up[l].astype(BF16), ffn2_down[l].astype(BF16),
                        row(ln3_g[l]), row(ln3_b[l]), alpha)
    return h.reshape(b, s, d)
```

```python
import functools
import math

import jax
import jax.numpy as jnp
from jax import lax
from jax.experimental import pallas as pl
from jax.experimental.pallas import tpu as pltpu

F32 = jnp.float32
BF16 = jnp.bfloat16

D_MODEL = 1024
HEAD_DIM = 64
N_ATTN_HEADS = 12
D_ATTN = N_ATTN_HEADS * HEAD_DIM
DILATED_BRANCHES = ((128, 1), (512, 4), (2048, 16))
ATTN_BLOCK = 128
ROPE_THETA = 500000.0
ROPE_DIM = HEAD_DIM // 4
N_SSD_HEADS = 12
SSD_HEAD_DIM = 64
D_SSD = N_SSD_HEADS * SSD_HEAD_DIM
SSD_GROUPS = 4
SSD_HEADS_PER_GROUP = N_SSD_HEADS // SSD_GROUPS
SSD_STATE = 128
CONV_WIDTH = 4
SSD_CHUNK = 128
D_BC = SSD_GROUPS * SSD_STATE
D_CONV = D_SSD + 2 * D_BC
D_FF = 2816
LN_EPS = 1e-5
RMS_EPS = 1e-6

LANES = 128
SUBLANES = 8
N_PAIRS = D_ATTN // LANES
DT_PAD = LANES
OFF_Q, OFF_K, OFF_V = 0, D_ATTN, 2 * D_ATTN
OFF_Z = 3 * D_ATTN
OFF_XBC = OFF_Z + D_SSD
OFF_DT = OFF_XBC + D_CONV
NEG = -1e30
LOG2E = math.log2(math.e)

TM = 512
TM_FFN = 1024
FF_CHUNK = 256
SUB_ROWS = 256
SSD_STEP = 512
MERGE_UNROLL = 4
VMEM_LIMIT = 56 * 1024 * 1024

assert all(w // d == ATTN_BLOCK for w, d in DILATED_BRANCHES)
DILATIONS = tuple(d for _, d in DILATED_BRANCHES)
assert DILATIONS == (1, 4, 16)
STRIDE = DILATIONS[1]


def _resident(shape):
    return pl.BlockSpec(shape, lambda *_: (0,) * len(shape), pipeline_mode=pl.Buffered(1))


def _layer_norm(t, g, b):
    mu = jnp.mean(t, axis=-1, keepdims=True)
    c = t - mu
    var = jnp.mean(c * c, axis=-1, keepdims=True)
    return c * lax.rsqrt(var + LN_EPS) * g + b


def _swiglu(xb, wg_ref, wu_ref, wd_ref, act_ref, rows):
    for c in range(D_FF // FF_CHUNK):
        sl = slice(c * FF_CHUNK, (c + 1) * FF_CHUNK)
        g = jnp.dot(xb, wg_ref[:, sl], preferred_element_type=F32)
        u = jnp.dot(xb, wu_ref[:, sl], preferred_element_type=F32)
        act_ref[rows, sl] = (jax.nn.silu(g) * u).astype(BF16)
    return jnp.dot(act_ref[rows, :], wd_ref[...], preferred_element_type=F32)


def _sub_tiles(tm):
    return [slice(r, r + SUB_ROWS) for r in range(0, tm, SUB_ROWS)]


def _ffn_ln_kernel(alpha, x_ref, wg_ref, wu_ref, wd_ref, g_ref, b_ref, o_ref, act_ref):
    for rows in _sub_tiles(x_ref.shape[0]):
        x = x_ref[rows, :]
        f = _swiglu(x.astype(BF16), wg_ref, wu_ref, wd_ref, act_ref, rows)
        o_ref[rows, :] = _layer_norm(alpha * x + 0.5 * f, g_ref[...], b_ref[...])


def _ffn_ln(x, wg, wu, wd, g, b, alpha):
    n = x.shape[0]
    row = pl.BlockSpec((TM_FFN, D_MODEL), lambda i: (i, 0))
    return pl.pallas_call(
        functools.partial(_ffn_ln_kernel, alpha),
        grid=(n // TM_FFN,),
        in_specs=[row, _resident((D_MODEL, D_FF)), _resident((D_MODEL, D_FF)),
                  _resident((D_FF, D_MODEL)), _resident((1, D_MODEL)), _resident((1, D_MODEL))],
        out_specs=row,
        out_shape=jax.ShapeDtypeStruct((n, D_MODEL), F32),
        scratch_shapes=[pltpu.VMEM((TM_FFN, D_FF), BF16)],
        compiler_params=pltpu.CompilerParams(
            dimension_semantics=("parallel",), vmem_limit_bytes=VMEM_LIMIT),
        name="ffn_ln",
    )(x, wg, wu, wd, g, b)


def _in_proj_kernel(h_ref, pos_ref, freq_ref, w_ref, wdt_ref, dtb_ref,
                    q1, q4, q16, k1, k4, k16, v1, v4, v16, z_ref, xbc_ref, dt_ref,
                    nat_ref, s4_ref):
    tm = h_ref.shape[0]
    hb = h_ref[...].astype(BF16)
    half = ROPE_DIM // 2
    q4n = tm // STRIDE
    q16n = q4n // STRIDE

    ang = freq_ref[...] * pos_ref[...].astype(F32)
    c8, s8 = jnp.cos(ang), jnp.sin(ang)
    z8 = jnp.zeros_like(c8)
    rest = HEAD_DIM - ROPE_DIM
    heads = LANES // HEAD_DIM
    cos_t = jnp.concatenate([c8, c8, jnp.ones((rest, tm), F32)] * heads, axis=0).T
    sin_lo = jnp.concatenate([s8, z8, jnp.zeros((rest, tm), F32)] * heads, axis=0).T
    sin_hi = jnp.concatenate([z8, s8, jnp.zeros((rest, tm), F32)] * heads, axis=0).T

    def project(slot, off, scale, rotate, r1):
        t = jnp.dot(hb, w_ref[:, off:off + D_ATTN], preferred_element_type=F32)
        for p in range(N_PAIRS):
            tp = t[:, p * LANES:(p + 1) * LANES]
            if rotate:
                tp = (tp * cos_t - pltpu.roll(tp, LANES - half, 1) * sin_lo
                      + pltpu.roll(tp, half, 1) * sin_hi)
            if scale != 1.0:
                tp = tp * scale
            nat_ref[slot * N_PAIRS + p] = tp
            r1[p] = tp.astype(BF16)

    def reorder(slot, r4, r16):
        for p in range(N_PAIRS):
            i = slot * N_PAIRS + p
            for r in range(STRIDE):
                t4 = nat_ref[i, pl.ds(r, q4n, stride=STRIDE), :]
                s4_ref[i, r * q4n:(r + 1) * q4n, :] = t4
                r4[p, :, r * LANES:(r + 1) * LANES] = t4.astype(BF16)
            for r in range(STRIDE):
                for a in range(STRIDE):
                    r16c = a * STRIDE + r
                    r16[p, :, r16c * LANES:(r16c + 1) * LANES] = s4_ref[
                        i, pl.ds(r * q4n + a, q16n, stride=STRIDE), :].astype(BF16)

    project(0, OFF_Q, LOG2E * HEAD_DIM ** -0.5, True, q1)
    project(1, OFF_K, 1.0, True, k1)
    reorder(0, q4, q16)
    project(2, OFF_V, 1.0, False, v1)
    reorder(1, k4, k16)
    z_ref[...] = jnp.dot(hb, w_ref[:, OFF_Z:OFF_Z + D_SSD],
                         preferred_element_type=F32).astype(BF16)
    reorder(2, v4, v16)
    xbc_ref[...] = jnp.dot(hb, w_ref[:, OFF_XBC:OFF_XBC + D_CONV], preferred_element_type=F32)
    dt = jnp.dot(hb, wdt_ref[...], preferred_element_type=F32)
    dt_ref[...] = jax.nn.softplus(dt + dtb_ref[...])


def _in_proj(h, pos, freq, w, wdt, dtb, b, s):
    n = h.shape[0]
    nt = s // TM
    row = lambda width: pl.BlockSpec((TM, width), lambda i, j: (i * nt + j, 0))
    qkv_shapes, qkv_specs = [], []
    for _ in range(3):
        for d in DILATIONS:
            qkv_shapes.append(jax.ShapeDtypeStruct((b, N_PAIRS, s // d, d * LANES), BF16))
            qkv_specs.append(pl.BlockSpec((None, N_PAIRS, TM // d, d * LANES),
                                          lambda i, j: (i, 0, j, 0)))
    return pl.pallas_call(
        _in_proj_kernel,
        grid=(b, nt),
        in_specs=[row(D_MODEL), pl.BlockSpec((None, 1, TM), lambda i, j: (i * nt + j, 0, 0)),
                  _resident((ROPE_DIM // 2, 1)), _resident((D_MODEL, OFF_DT)),
                  _resident((D_MODEL, DT_PAD)),
                  _resident((1, DT_PAD))],
        out_specs=qkv_specs + [row(D_SSD), row(D_CONV), row(DT_PAD)],
        out_shape=qkv_shapes + [jax.ShapeDtypeStruct((n, D_SSD), BF16),
                                jax.ShapeDtypeStruct((n, D_CONV), F32),
                                jax.ShapeDtypeStruct((n, DT_PAD), F32)],
        scratch_shapes=[pltpu.VMEM((3 * N_PAIRS, TM, LANES), F32),
                        pltpu.VMEM((3 * N_PAIRS, TM, LANES), F32)],
        compiler_params=pltpu.CompilerParams(
            dimension_semantics=("parallel", "parallel"), vmem_limit_bytes=VMEM_LIMIT),
        name="in_proj",
    )(h, pos, freq, w, wdt, dtb)


def _attn_kernel(q1, k1, v1, q4, k4, v4, q16, k16, v16, nw_ref, o_ref,
                 va1, vb1, va4, vb4, va16, vb16, bias_ref, res1, res4, res16, acc):
    hp = pl.program_id(1)
    s = q1.shape[0]
    blk = ATTN_BLOCK
    sq = s // STRIDE
    lo = lax.broadcasted_iota(jnp.int32, (blk, LANES), 1) < HEAD_DIM

    wrow = lax.broadcasted_iota(jnp.int32, (2 * blk, 2 * blk), 0) % blk
    wcol = lax.broadcasted_iota(jnp.int32, (2 * blk, 2 * blk), 1)
    keep = ((wcol < blk) & (wcol >= wrow)) | ((wcol >= blk) & (wcol - blk <= wrow))
    bias_ref[...] = jnp.where(keep, 0.0, NEG)

    for v, va, vb in ((v1, va1, vb1), (v4, va4, vb4), (v16, va16, vb16)):
        first = lax.broadcasted_iota(jnp.int32, v.shape, 1) % LANES < HEAD_DIM
        vv = v[...]
        one = jnp.ones_like(vv)
        va[...] = jnp.where(first, vv, one)
        vb[...] = jnp.where(first, one, vv)

    def block(q, kw, vaw, vbw, bias):
        zero = jnp.zeros_like(q)
        qs = jnp.concatenate([jnp.where(lo, q, zero), jnp.where(lo, zero, q)], axis=0)
        sc = lax.dot_general(qs, kw, (((1,), (1,)), ((), ())), preferred_element_type=F32) + bias
        m = jnp.max(sc, axis=-1, keepdims=True)
        p = jnp.exp2(sc - m).astype(BF16)
        ra = jnp.dot(p[:blk], vaw, preferred_element_type=F32)
        rb = jnp.dot(p[blk:], vbw, preferred_element_type=F32)
        oc = jnp.where(lo, ra, rb)
        lx = jnp.where(lo, rb, ra)
        mc = jnp.where(lo, jnp.broadcast_to(m[:blk], (blk, LANES)),
                       jnp.broadcast_to(m[blk:], (blk, LANES)))
        return oc, mc, lx

    def run_branch(d, qr, kr, var, vbr, put):
        nblk = s // d // blk
        for r in range(d):
            ls = slice(r * LANES, (r + 1) * LANES)
            for i in range(nblk):
                k0 = max(i - 1, 0) * blk
                bias = bias_ref[...] if i > 0 else bias_ref[:, blk:]
                put(r, i, block(qr[i * blk:(i + 1) * blk, ls], kr[k0:(i + 1) * blk, ls],
                                var[k0:(i + 1) * blk, ls], vbr[k0:(i + 1) * blk, ls], bias))

    def put1(r, i, vals):
        for j, val in enumerate(vals):
            res1[j, i * blk:(i + 1) * blk, :] = val

    def put4(r, i, vals):
        for j, val in enumerate(vals):
            res4[j, r * sq + i * blk:r * sq + (i + 1) * blk, :] = val

    def put16(r, i, vals):
        a, r4 = divmod(r, STRIDE)
        for j, val in enumerate(vals):
            res16[j, pl.ds(r4 * sq + a, blk, stride=STRIDE), :] = val

    run_branch(DILATIONS[0], q1, k1, va1, vb1, put1)
    run_branch(DILATIONS[1], q4, k4, va4, vb4, put4)
    run_branch(DILATIONS[2], q16, k16, va16, vb16, put16)

    nj = sq // blk

    def merge(it, carry):
        r4, jb = it // nj, it % nj
        nat = pl.ds(jb * (STRIDE * blk) + r4, blk, stride=STRIDE)
        rows = pl.ds(pl.multiple_of(r4 * sq + jb * blk, blk), blk)
        o = (res1[0, nat, :], res4[0, rows, :], res16[0, rows, :])
        m = (res1[1, nat, :], res4[1, rows, :], res16[1, rows, :])
        l = (res1[2, nat, :], res4[2, rows, :], res16[2, rows, :])
        mx = jnp.maximum(jnp.maximum(m[0], m[1]), m[2])
        e = [jnp.exp2(mi - mx) for mi in m]
        num = e[0] * o[0] + e[1] * o[1] + e[2] * o[2]
        den = sum(ei * pltpu.roll(li, HEAD_DIM, 1) for ei, li in zip(e, l))
        acc[hp, nat, :] = num / den
        return carry
    lax.fori_loop(0, STRIDE * nj, merge, 0, unroll=MERGE_UNROLL)

    @pl.when(hp == N_PAIRS - 1)
    def _():
        def norm(i, carry):
            rows = pl.ds(pl.multiple_of(i * blk, blk), blk)
            sq_sum = acc[0, rows, :] * acc[0, rows, :]
            for p in range(1, N_PAIRS):
                sq_sum = sq_sum + acc[p, rows, :] * acc[p, rows, :]
            inv = lax.rsqrt(jnp.sum(sq_sum, axis=-1, keepdims=True) / D_ATTN + RMS_EPS)
            for p in range(N_PAIRS):
                ps = slice(p * LANES, (p + 1) * LANES)
                o_ref[rows, ps] = (acc[p, rows, :] * inv * nw_ref[:, ps]).astype(BF16)
            return carry
        lax.fori_loop(0, s // blk, norm, 0, unroll=2)


def _attention(qkv, nw, b, s):
    in_specs = []
    for _ in range(3):
        for d in DILATIONS:
            in_specs.append(pl.BlockSpec((None, None, s // d, d * LANES), lambda i, p: (i, p, 0, 0)))
    ops = [qkv[t * 3 + g] for g in range(3) for t in range(3)]
    specs = [in_specs[t * 3 + g] for g in range(3) for t in range(3)]
    vscratch = []
    for d in DILATIONS:
        vscratch += [pltpu.VMEM((s // d, d * LANES), BF16)] * 2
    return pl.pallas_call(
        _attn_kernel,
        grid=(b, N_PAIRS),
        in_specs=specs + [pl.BlockSpec((1, D_ATTN), lambda i, p: (0, 0))],
        out_specs=pl.BlockSpec((None, s, D_ATTN), lambda i, p: (i, 0, 0)),
        out_shape=jax.ShapeDtypeStruct((b, s, D_ATTN), BF16),
        scratch_shapes=vscratch + [
            pltpu.VMEM((2 * ATTN_BLOCK, 2 * ATTN_BLOCK), F32),
            pltpu.VMEM((3, s, LANES), F32), pltpu.VMEM((3, s, LANES), F32),
            pltpu.VMEM((3, s, LANES), F32), pltpu.VMEM((N_PAIRS, s, LANES), F32)],
        compiler_params=pltpu.CompilerParams(
            dimension_semantics=("parallel", "arbitrary"), vmem_limit_bytes=VMEM_LIMIT),
        name="dilated_attention",
    )(*ops, nw)


def _ssd_kernel(xbc_ref, dt_ref, z_ref, cw_ref, cb_ref, alog_ref, dskip_ref, nw_ref, o_ref,
                ubuf, cbuf, hst, ybuf):
    t = xbc_ref.shape[0]
    cl = SSD_CHUNK
    halo = SUBLANES

    @pl.when(pl.program_id(1) == 0)
    def _():
        ubuf[0:halo, :] = jnp.zeros((halo, D_CONV), F32)
        hst[...] = jnp.zeros_like(hst)

    ubuf[halo:halo + t, :] = xbc_ref[...]
    assert CONV_WIDTH == 4
    u = ubuf[...]
    u1 = pltpu.roll(u, 1, 0)
    v2 = pltpu.roll(cw_ref[1:2, :] * u + cw_ref[0:1, :] * u1, 2, 0)
    conv = (cb_ref[...] + cw_ref[3:4, :] * u[halo:halo + t] + cw_ref[2:3, :] * u1[halo:halo + t]
            + v2[halo:halo + t])
    cbuf[...] = jax.nn.silu(conv)
    ubuf[0:halo, :] = ubuf[t:t + halo, :]

    lo = lax.broadcasted_iota(jnp.int32, (cl, LANES), 1) < SSD_HEAD_DIM
    ri = lax.broadcasted_iota(jnp.int32, (cl, cl), 0)
    ci = lax.broadcasted_iota(jnp.int32, (cl, cl), 1)
    tril = ri >= ci
    tril_f = tril.astype(F32)
    group_w = SSD_HEADS_PER_GROUP * SSD_HEAD_DIM
    a_row = -jnp.exp(alog_ref[...]) * LOG2E

    def expand(v):
        tiles = []
        for p in range(N_PAIRS):
            a = jnp.broadcast_to(v[:, 2 * p:2 * p + 1], (cl, LANES))
            b = jnp.broadcast_to(v[:, 2 * p + 1:2 * p + 2], (cl, LANES))
            tiles.append(jnp.where(lo, a, b))
        return jnp.concatenate(tiles, axis=1)

    def tile_groups(p):
        g0, g1 = (p * LANES) // group_w, ((p + 1) * LANES - 1) // group_w
        if g0 == g1:
            return (g0,), None
        first = lax.broadcasted_iota(jnp.int32, (cl, LANES), 1) < g1 * group_w - p * LANES
        return (g0, g1), first

    for c in range(t // cl):
        rows = slice(c * cl, (c + 1) * cl)
        dt = dt_ref[rows, :]
        acum = jnp.dot(tril_f, dt * a_row, preferred_element_type=F32,
                       precision=lax.Precision.HIGHEST)
        acum_t = acum.T
        dt_x = expand(dt)
        acum_x = expand(acum)
        xs = cbuf[rows, 0:D_SSD]
        bm = cbuf[rows, D_SSD:D_SSD + D_BC].astype(BF16)
        cm = cbuf[rows, D_SSD + D_BC:D_CONV].astype(BF16)
        bmg = [bm[:, g * SSD_STATE:(g + 1) * SSD_STATE] for g in range(SSD_GROUPS)]
        cmg = [cm[:, g * SSD_STATE:(g + 1) * SSD_STATE] for g in range(SSD_GROUPS)]
        xdt = xs * dt_x
        last = acum_x[cl - 1:cl, :]
        xdecay = (xdt * jnp.exp2(last - acum_x)).astype(BF16)
        xdt_b = xdt.astype(BF16)
        eacum = jnp.exp2(acum_x)
        cdecay = jnp.exp2(last)

        cbs = [lax.dot_general(cmg[g], bmg[g], (((1,), (1,)), ((), ())),
                               preferred_element_type=F32) for g in range(SSD_GROUPS)]

        def gmat(hd):
            seg = acum[:, hd:hd + 1] - acum_t[hd:hd + 1, :]
            return (cbs[hd // SSD_HEADS_PER_GROUP]
                    * jnp.exp2(jnp.where(tril, seg, NEG))).astype(BF16)

        for p in range(N_PAIRS):
            ps = slice(p * LANES, (p + 1) * LANES)
            xp = xdt_b[:, ps]
            zero = jnp.zeros_like(xp)
            xpair = jnp.concatenate([jnp.where(lo, xp, zero), jnp.where(lo, zero, xp)], axis=0)
            gpair = jnp.concatenate([gmat(2 * p), gmat(2 * p + 1)], axis=1)
            y = jnp.dot(gpair, xpair, preferred_element_type=F32)
            groups, first = tile_groups(p)
            hp_b = hst[:, ps].astype(BF16)
            offs = [jnp.dot(cmg[g], hp_b, preferred_element_type=F32) for g in groups]
            sts = [lax.dot_general(bmg[g], xdecay[:, ps], (((0,), (0,)), ((), ())),
                                   preferred_element_type=F32) for g in groups]
            off = offs[0] if first is None else jnp.where(first, offs[0], offs[1])
            st = sts[0] if first is None else jnp.where(first, sts[0], sts[1])
            ybuf[:, ps] = y + off * eacum[:, ps]
            hst[:, ps] = hst[:, ps] * cdecay[:, ps] + st

        y = ybuf[...] + dskip_ref[...] * xs
        y = y * jax.nn.silu(z_ref[rows, :].astype(F32))
        inv = lax.rsqrt(jnp.mean(y * y, axis=-1, keepdims=True) + RMS_EPS)
        o_ref[rows, :] = (y * inv * nw_ref[...]).astype(BF16)


def _ssd(xbc, dt, z, cw, cb, alog, dskip, nw, b, s):
    n = xbc.shape[0]
    nt = s // SSD_STEP
    row = lambda width: pl.BlockSpec((SSD_STEP, width), lambda i, j: (i * nt + j, 0))
    const = lambda shape: pl.BlockSpec(shape, lambda i, j: (0, 0))
    return pl.pallas_call(
        _ssd_kernel,
        grid=(b, nt),
        in_specs=[row(D_CONV), row(DT_PAD), row(D_SSD), const((CONV_WIDTH, D_CONV)),
                  const((1, D_CONV)), const((1, DT_PAD)), const((1, D_SSD)), const((1, D_SSD))],
        out_specs=row(D_SSD),
        out_shape=jax.ShapeDtypeStruct((n, D_SSD), BF16),
        scratch_shapes=[pltpu.VMEM((SSD_STEP + SUBLANES, D_CONV), F32),
                        pltpu.VMEM((SSD_STEP, D_CONV), F32),
                        pltpu.VMEM((SSD_STATE, D_SSD), F32), pltpu.VMEM((SSD_CHUNK, D_SSD), F32)],
        compiler_params=pltpu.CompilerParams(
            dimension_semantics=("parallel", "arbitrary"), vmem_limit_bytes=VMEM_LIMIT),
        name="ssd",
    )(xbc, dt, z, cw, cb, alog, dskip, nw)


def _out_ffn_ln_kernel(alpha, a_ref, y_ref, h_ref, wo_ref, g2_ref, b2_ref,
                       wg_ref, wu_ref, wd_ref, g3_ref, b3_ref, o_ref, act_ref):
    rows = slice(0, h_ref.shape[0])
    mix = (jnp.dot(a_ref[...], wo_ref[0:D_ATTN, :], preferred_element_type=F32)
           + jnp.dot(y_ref[...], wo_ref[D_ATTN:D_ATTN + D_SSD, :], preferred_element_type=F32))
    h2 = _layer_norm(alpha * h_ref[...] + mix, g2_ref[...], b2_ref[...])
    f = _swiglu(h2.astype(BF16), wg_ref, wu_ref, wd_ref, act_ref, rows)
    o_ref[...] = _layer_norm(alpha * h2 + 0.5 * f, g3_ref[...], b3_ref[...])


def _out_ffn_ln(attn, y, h, wo, g2, b2, wg, wu, wd, g3, b3, alpha):
    n = h.shape[0]
    row = lambda width: pl.BlockSpec((TM_FFN, width), lambda i: (i, 0))
    vec = _resident((1, D_MODEL))
    return pl.pallas_call(
        functools.partial(_out_ffn_ln_kernel, alpha),
        grid=(n // TM_FFN,),
        in_specs=[row(D_ATTN), row(D_SSD), row(D_MODEL), _resident((D_ATTN + D_SSD, D_MODEL)),
                  vec, vec, _resident((D_MODEL, D_FF)), _resident((D_MODEL, D_FF)),
                  _resident((D_FF, D_MODEL)), vec, vec],
        out_specs=row(D_MODEL),
        out_shape=jax.ShapeDtypeStruct((n, D_MODEL), F32),
        scratch_shapes=[pltpu.VMEM((TM_FFN, D_FF), BF16)],
        compiler_params=pltpu.CompilerParams(
            dimension_semantics=("parallel",), vmem_limit_bytes=VMEM_LIMIT),
        name="out_ffn_ln",
    )(attn, y, h, wo, g2, b2, wg, wu, wd, g3, b3)


def kernel(x, positions, ln1_g, ln1_b, ffn1_gate, ffn1_up, ffn1_down, w_in, conv_w, conv_b, dt_bias, a_log, d_skip, attn_norm_w, ssd_norm_w, w_out, ln2_g, ln2_b, ffn2_gate, ffn2_up, ffn2_down, ln3_g, ln3_b):
    b, s, d = x.shape
    depth = w_in.shape[0]
    assert d == D_MODEL and s % TM == 0 and s % (DILATIONS[-1] * ATTN_BLOCK) == 0
    assert (b * s) % TM_FFN == 0
    n = b * s
    alpha = (2.0 * depth) ** 0.25
    row = lambda v: v.reshape(1, -1).astype(F32)
    pad_lanes = lambda v: jnp.pad(row(v), ((0, 0), (0, LANES - v.shape[-1])))

    inv_freq = ROPE_THETA ** (-jnp.arange(0, ROPE_DIM, 2, dtype=F32) / ROPE_DIM)
    freq = inv_freq.reshape(ROPE_DIM // 2, 1)
    pos = positions.reshape(n // TM, 1, TM)
    h = x.reshape(n, d)
    for l in range(depth):
        h = _ffn_ln(h, ffn1_gate[l].astype(BF16), ffn1_up[l].astype(BF16), ffn1_down[l].astype(BF16),
                    row(ln1_g[l]), row(ln1_b[l]), alpha)
        w = w_in[l][:, :OFF_DT].astype(BF16)
        wdt = jnp.pad(w_in[l][:, OFF_DT:], ((0, 0), (0, DT_PAD - N_SSD_HEADS))).astype(BF16)
        *qkv, z, xbc, dt = _in_proj(h, pos, freq, w, wdt, pad_lanes(dt_bias[l]), b, s)
        attn = _attention(qkv, row(attn_norm_w[l]), b, s).reshape(n, D_ATTN)
        y = _ssd(xbc, dt, z, conv_w[l].astype(F32), row(conv_b[l]), pad_lanes(a_log[l]),
                 row(jnp.repeat(d_skip[l], SSD_HEAD_DIM)), row(ssd_norm_w[l]), b, s)
        h = _out_ffn_ln(attn, y, h, w_out[l].astype(BF16), row(ln2_g[l]), row(ln2_b[l]),
                        ffn2_gate[l].astype(BF16), ffn2_up[l].astype(BF16), ffn2_down[l].astype(BF16),
                        row(ln3_g[l]), row(ln3_b[l]), alpha)
    return h.reshape(b, s, d)
```

```python
import functools
import math

import jax
import jax.numpy as jnp
from jax import lax
from jax.experimental import pallas as pl
from jax.experimental.pallas import tpu as pltpu

F32 = jnp.float32
BF16 = jnp.bfloat16

D_MODEL = 1024
HEAD_DIM = 64
N_ATTN_HEADS = 12
D_ATTN = N_ATTN_HEADS * HEAD_DIM
DILATED_BRANCHES = ((128, 1), (512, 4), (2048, 16))
ATTN_BLOCK = 128
ROPE_THETA = 500000.0
ROPE_DIM = HEAD_DIM // 4
N_SSD_HEADS = 12
SSD_HEAD_DIM = 64
D_SSD = N_SSD_HEADS * SSD_HEAD_DIM
SSD_GROUPS = 4
SSD_HEADS_PER_GROUP = N_SSD_HEADS // SSD_GROUPS
SSD_STATE = 128
CONV_WIDTH = 4
SSD_CHUNK = 128
D_BC = SSD_GROUPS * SSD_STATE
D_CONV = D_SSD + 2 * D_BC
D_FF = 2816
LN_EPS = 1e-5
RMS_EPS = 1e-6

LANES = 128
SUBLANES = 8
N_PAIRS = D_ATTN // LANES
DT_PAD = LANES
OFF_Q, OFF_K, OFF_V = 0, D_ATTN, 2 * D_ATTN
OFF_Z = 3 * D_ATTN
OFF_XBC = OFF_Z + D_SSD
OFF_DT = OFF_XBC + D_CONV
NEG = -1e30
LOG2E = math.log2(math.e)

TM = 512
TM_FFN = 1024
FF_CHUNK = 256
SUB_ROWS = 256
SSD_STEP = 512
PROJ_COLS = 256
MERGE_UNROLL = 4
VMEM_LIMIT = 56 * 1024 * 1024

assert all(w // d == ATTN_BLOCK for w, d in DILATED_BRANCHES)
DILATIONS = tuple(d for _, d in DILATED_BRANCHES)
assert DILATIONS == (1, 4, 16)
STRIDE = DILATIONS[1]


def _resident(shape):
    return pl.BlockSpec(shape, lambda *_: (0,) * len(shape), pipeline_mode=pl.Buffered(1))


def _layer_norm(t, g, b):
    mu = jnp.mean(t, axis=-1, keepdims=True)
    c = t - mu
    var = jnp.mean(c * c, axis=-1, keepdims=True)
    return c * lax.rsqrt(var + LN_EPS) * g + b


def _swiglu(xb, wg_ref, wu_ref, wd_ref, act_ref, rows):
    for c in range(D_FF // FF_CHUNK):
        sl = slice(c * FF_CHUNK, (c + 1) * FF_CHUNK)
        g = jnp.dot(xb, wg_ref[:, sl], preferred_element_type=F32)
        u = jnp.dot(xb, wu_ref[:, sl], preferred_element_type=F32)
        act_ref[rows, sl] = (jax.nn.silu(g) * u).astype(BF16)
    return jnp.dot(act_ref[rows, :], wd_ref[...], preferred_element_type=F32)


def _sub_tiles(tm):
    return [slice(r, r + SUB_ROWS) for r in range(0, tm, SUB_ROWS)]


def _ffn_ln_kernel(alpha, x_ref, wg_ref, wu_ref, wd_ref, g_ref, b_ref, o_ref, act_ref):
    for rows in _sub_tiles(x_ref.shape[0]):
        x = x_ref[rows, :]
        f = _swiglu(x.astype(BF16), wg_ref, wu_ref, wd_ref, act_ref, rows)
        o_ref[rows, :] = _layer_norm(alpha * x + 0.5 * f, g_ref[...], b_ref[...])


def _ffn_ln(x, wg, wu, wd, g, b, alpha):
    n = x.shape[0]
    row = pl.BlockSpec((TM_FFN, D_MODEL), lambda i: (i, 0))
    return pl.pallas_call(
        functools.partial(_ffn_ln_kernel, alpha),
        grid=(n // TM_FFN,),
        in_specs=[row, _resident((D_MODEL, D_FF)), _resident((D_MODEL, D_FF)),
                  _resident((D_FF, D_MODEL)), _resident((1, D_MODEL)), _resident((1, D_MODEL))],
        out_specs=row,
        out_shape=jax.ShapeDtypeStruct((n, D_MODEL), F32),
        scratch_shapes=[pltpu.VMEM((TM_FFN, D_FF), BF16)],
        compiler_params=pltpu.CompilerParams(
            dimension_semantics=("parallel",), vmem_limit_bytes=VMEM_LIMIT),
        name="ffn_ln",
    )(x, wg, wu, wd, g, b)


def _in_proj_kernel(h_ref, pos_ref, freq_ref, w_ref,
                    q1, q4, q16, k1, k4, k16, v1, v4, v16, nat_ref, s4_ref):
    tm = h_ref.shape[0]
    hb = h_ref[...].astype(BF16)
    half = ROPE_DIM // 2
    q4n = tm // STRIDE
    q16n = q4n // STRIDE

    ang = freq_ref[...] * pos_ref[...].astype(F32)
    c8, s8 = jnp.cos(ang), jnp.sin(ang)
    z8 = jnp.zeros_like(c8)
    rest = HEAD_DIM - ROPE_DIM
    heads = LANES // HEAD_DIM
    cos_t = jnp.concatenate([c8, c8, jnp.ones((rest, tm), F32)] * heads, axis=0).T
    sin_lo = jnp.concatenate([s8, z8, jnp.zeros((rest, tm), F32)] * heads, axis=0).T
    sin_hi = jnp.concatenate([z8, s8, jnp.zeros((rest, tm), F32)] * heads, axis=0).T

    def project(slot, off, scale, rotate, r1):
        t = jnp.dot(hb, w_ref[:, off:off + D_ATTN], preferred_element_type=F32)
        for p in range(N_PAIRS):
            tp = t[:, p * LANES:(p + 1) * LANES]
            if rotate:
                tp = (tp * cos_t - pltpu.roll(tp, LANES - half, 1) * sin_lo
                      + pltpu.roll(tp, half, 1) * sin_hi)
            if scale != 1.0:
                tp = tp * scale
            nat_ref[slot * N_PAIRS + p] = tp
            r1[p] = tp.astype(BF16)

    def reorder(slot, r4, r16):
        for p in range(N_PAIRS):
            i = slot * N_PAIRS + p
            for r in range(STRIDE):
                t4 = nat_ref[i, pl.ds(r, q4n, stride=STRIDE), :]
                s4_ref[i, r * q4n:(r + 1) * q4n, :] = t4
                r4[p, :, r * LANES:(r + 1) * LANES] = t4.astype(BF16)
            for r in range(STRIDE):
                for a in range(STRIDE):
                    r16c = a * STRIDE + r
                    r16[p, :, r16c * LANES:(r16c + 1) * LANES] = s4_ref[
                        i, pl.ds(r * q4n + a, q16n, stride=STRIDE), :].astype(BF16)

    project(0, OFF_Q, LOG2E * HEAD_DIM ** -0.5, True, q1)
    project(1, OFF_K, 1.0, True, k1)
    reorder(0, q4, q16)
    project(2, OFF_V, 1.0, False, v1)
    reorder(1, k4, k16)
    reorder(2, v4, v16)


def _in_proj(h, pos, freq, w, b, s):
    nt = s // TM
    row = lambda width: pl.BlockSpec((TM, width), lambda i, j: (i * nt + j, 0))
    qkv_shapes, qkv_specs = [], []
    for _ in range(3):
        for d in DILATIONS:
            qkv_shapes.append(jax.ShapeDtypeStruct((b, N_PAIRS, s // d, d * LANES), BF16))
            qkv_specs.append(pl.BlockSpec((None, N_PAIRS, TM // d, d * LANES),
                                          lambda i, j: (i, 0, j, 0)))
    return pl.pallas_call(
        _in_proj_kernel,
        grid=(b, nt),
        in_specs=[row(D_MODEL), pl.BlockSpec((None, 1, TM), lambda i, j: (i * nt + j, 0, 0)),
                  _resident((ROPE_DIM // 2, 1)), _resident((D_MODEL, OFF_Z))],
        out_specs=qkv_specs,
        out_shape=qkv_shapes,
        scratch_shapes=[pltpu.VMEM((3 * N_PAIRS, TM, LANES), F32),
                        pltpu.VMEM((3 * N_PAIRS, TM, LANES), F32)],
        compiler_params=pltpu.CompilerParams(
            dimension_semantics=("parallel", "parallel"), vmem_limit_bytes=VMEM_LIMIT),
        name="in_proj",
    )(h, pos, freq, w)


def _attn_kernel(q1, k1, v1, q4, k4, v4, q16, k16, v16, nw_ref, o_ref,
                 va1, vb1, va4, vb4, va16, vb16, bias_ref, res1, res4, res16, acc):
    hp = pl.program_id(1)
    s = q1.shape[0]
    blk = ATTN_BLOCK
    sq = s // STRIDE
    lo = lax.broadcasted_iota(jnp.int32, (blk, LANES), 1) < HEAD_DIM

    wrow = lax.broadcasted_iota(jnp.int32, (2 * blk, 2 * blk), 0) % blk
    wcol = lax.broadcasted_iota(jnp.int32, (2 * blk, 2 * blk), 1)
    keep = ((wcol < blk) & (wcol >= wrow)) | ((wcol >= blk) & (wcol - blk <= wrow))
    bias_ref[...] = jnp.where(keep, 0.0, NEG)

    for v, va, vb in ((v1, va1, vb1), (v4, va4, vb4), (v16, va16, vb16)):
        first = lax.broadcasted_iota(jnp.int32, v.shape, 1) % LANES < HEAD_DIM
        vv = v[...]
        one = jnp.ones_like(vv)
        va[...] = jnp.where(first, vv, one)
        vb[...] = jnp.where(first, one, vv)

    def block(q, kw, vaw, vbw, bias):
        zero = jnp.zeros_like(q)
        qs = jnp.concatenate([jnp.where(lo, q, zero), jnp.where(lo, zero, q)], axis=0)
        sc = lax.dot_general(qs, kw, (((1,), (1,)), ((), ())), preferred_element_type=F32) + bias
        m = jnp.max(sc, axis=-1, keepdims=True)
        p = jnp.exp2(sc - m).astype(BF16)
        ra = jnp.dot(p[:blk], vaw, preferred_element_type=F32)
        rb = jnp.dot(p[blk:], vbw, preferred_element_type=F32)
        oc = jnp.where(lo, ra, rb)
        lx = jnp.where(lo, rb, ra)
        mc = jnp.where(lo, jnp.broadcast_to(m[:blk], (blk, LANES)),
                       jnp.broadcast_to(m[blk:], (blk, LANES)))
        return oc, mc, lx

    def run_branch(d, qr, kr, var, vbr, put):
        nblk = s // d // blk
        for r in range(d):
            ls = slice(r * LANES, (r + 1) * LANES)
            for i in range(nblk):
                k0 = max(i - 1, 0) * blk
                bias = bias_ref[...] if i > 0 else bias_ref[:, blk:]
                put(r, i, block(qr[i * blk:(i + 1) * blk, ls], kr[k0:(i + 1) * blk, ls],
                                var[k0:(i + 1) * blk, ls], vbr[k0:(i + 1) * blk, ls], bias))

    def put1(r, i, vals):
        for j, val in enumerate(vals):
            res1[j, i * blk:(i + 1) * blk, :] = val

    def put4(r, i, vals):
        for j, val in enumerate(vals):
            res4[j, r * sq + i * blk:r * sq + (i + 1) * blk, :] = val

    def put16(r, i, vals):
        a, r4 = divmod(r, STRIDE)
        for j, val in enumerate(vals):
            res16[j, pl.ds(r4 * sq + a, blk, stride=STRIDE), :] = val

    run_branch(DILATIONS[0], q1, k1, va1, vb1, put1)
    run_branch(DILATIONS[1], q4, k4, va4, vb4, put4)
    run_branch(DILATIONS[2], q16, k16, va16, vb16, put16)

    nj = sq // blk

    def merge(it, carry):
        r4, jb = it // nj, it % nj
        nat = pl.ds(jb * (STRIDE * blk) + r4, blk, stride=STRIDE)
        rows = pl.ds(pl.multiple_of(r4 * sq + jb * blk, blk), blk)
        o = (res1[0, nat, :], res4[0, rows, :], res16[0, rows, :])
        m = (res1[1, nat, :], res4[1, rows, :], res16[1, rows, :])
        l = (res1[2, nat, :], res4[2, rows, :], res16[2, rows, :])
        mx = jnp.maximum(jnp.maximum(m[0], m[1]), m[2])
        e = [jnp.exp2(mi - mx) for mi in m]
        num = e[0] * o[0] + e[1] * o[1] + e[2] * o[2]
        den = sum(ei * pltpu.roll(li, HEAD_DIM, 1) for ei, li in zip(e, l))
        acc[hp, nat, :] = num / den
        return carry
    lax.fori_loop(0, STRIDE * nj, merge, 0, unroll=MERGE_UNROLL)

    @pl.when(hp == N_PAIRS - 1)
    def _():
        def norm(i, carry):
            rows = pl.ds(pl.multiple_of(i * blk, blk), blk)
            sq_sum = acc[0, rows, :] * acc[0, rows, :]
            for p in range(1, N_PAIRS):
                sq_sum = sq_sum + acc[p, rows, :] * acc[p, rows, :]
            inv = lax.rsqrt(jnp.sum(sq_sum, axis=-1, keepdims=True) / D_ATTN + RMS_EPS)
            for p in range(N_PAIRS):
                ps = slice(p * LANES, (p + 1) * LANES)
                o_ref[rows, ps] = (acc[p, rows, :] * inv * nw_ref[:, ps]).astype(BF16)
            return carry
        lax.fori_loop(0, s // blk, norm, 0, unroll=2)


def _attention(qkv, nw, b, s):
    in_specs = []
    for _ in range(3):
        for d in DILATIONS:
            in_specs.append(pl.BlockSpec((None, None, s // d, d * LANES), lambda i, p: (i, p, 0, 0)))
    ops = [qkv[t * 3 + g] for g in range(3) for t in range(3)]
    specs = [in_specs[t * 3 + g] for g in range(3) for t in range(3)]
    vscratch = []
    for d in DILATIONS:
        vscratch += [pltpu.VMEM((s // d, d * LANES), BF16)] * 2
    return pl.pallas_call(
        _attn_kernel,
        grid=(b, N_PAIRS),
        in_specs=specs + [pl.BlockSpec((1, D_ATTN), lambda i, p: (0, 0))],
        out_specs=pl.BlockSpec((None, s, D_ATTN), lambda i, p: (i, 0, 0)),
        out_shape=jax.ShapeDtypeStruct((b, s, D_ATTN), BF16),
        scratch_shapes=vscratch + [
            pltpu.VMEM((2 * ATTN_BLOCK, 2 * ATTN_BLOCK), F32),
            pltpu.VMEM((3, s, LANES), F32), pltpu.VMEM((3, s, LANES), F32),
            pltpu.VMEM((3, s, LANES), F32), pltpu.VMEM((N_PAIRS, s, LANES), F32)],
        compiler_params=pltpu.CompilerParams(
            dimension_semantics=("parallel", "arbitrary"), vmem_limit_bytes=VMEM_LIMIT),
        name="dilated_attention",
    )(*ops, nw)


def _ssd_kernel(nt, h0_ref, ha_ref, hb_ref, w_ref, wdt_ref, dtb_ref, cw_ref, cb_ref, alog_ref,
                dskip_ref, nw_ref, o_ref, ubuf0, ubuf1, zbuf0, zbuf1, dtbuf0, dtbuf1, cbuf0, cbuf1,
                hbuf0, hbuf1, hst):
    g = pl.program_id(0)
    t = SSD_STEP
    cl = SSD_CHUNK
    halo = SUBLANES
    assert CONV_WIDTH == 4

    lo = lax.broadcasted_iota(jnp.int32, (cl, LANES), 1) < SSD_HEAD_DIM
    ri = lax.broadcasted_iota(jnp.int32, (cl, cl), 0)
    ci = lax.broadcasted_iota(jnp.int32, (cl, cl), 1)
    tril = ri >= ci
    tril_f = tril.astype(F32)
    group_w = SSD_HEADS_PER_GROUP * SSD_HEAD_DIM
    a_row = -jnp.exp(alog_ref[...]) * LOG2E

    def project_pieces(h_ref, hbuf, ubuf, zbuf, dtbuf):
        def cast():
            hbuf[...] = h_ref[...].astype(BF16)

        def xbc(k):
            cols = slice(k * PROJ_COLS, (k + 1) * PROJ_COLS)
            wcols = slice(D_SSD + k * PROJ_COLS, D_SSD + (k + 1) * PROJ_COLS)
            ubuf[halo:halo + t, cols] = jnp.dot(hbuf[...], w_ref[:, wcols],
                                                preferred_element_type=F32)

        def z(k):
            cols = slice(k * PROJ_COLS, (k + 1) * PROJ_COLS)
            zbuf[:, cols] = jnp.dot(hbuf[...], w_ref[:, cols], preferred_element_type=F32)

        def dt():
            d = jnp.dot(hbuf[...], wdt_ref[...], preferred_element_type=F32)
            dtbuf[...] = jax.nn.softplus(d + dtb_ref[...])

        return ([cast] + [functools.partial(xbc, k) for k in range(D_CONV // PROJ_COLS)],
                [functools.partial(z, k) for k in range(D_SSD // PROJ_COLS)] + [dt])

    def expand(v):
        tiles = []
        for p in range(N_PAIRS):
            a = jnp.broadcast_to(v[:, 2 * p:2 * p + 1], (cl, LANES))
            b = jnp.broadcast_to(v[:, 2 * p + 1:2 * p + 2], (cl, LANES))
            tiles.append(jnp.where(lo, a, b))
        return jnp.concatenate(tiles, axis=1)

    def tile_groups(p):
        g0, g1 = (p * LANES) // group_w, ((p + 1) * LANES - 1) // group_w
        if g0 == g1:
            return (g0,), None
        first = lax.broadcasted_iota(jnp.int32, (cl, LANES), 1) < g1 * group_w - p * LANES
        return (g0, g1), first

    def begin(tile, ubuf, prev_ubuf):
        ubuf[0:halo, :] = prev_ubuf[t:t + halo, :]

        @pl.when(tile % nt == 0)
        def _():
            ubuf[0:halo, :] = jnp.zeros((halo, D_CONV), F32)
            hst[...] = jnp.zeros_like(hst)

    def conv_piece(ubuf, cbuf, k):
        cols = slice(k * PROJ_COLS, (k + 1) * PROJ_COLS)
        u = ubuf[:, cols]
        u1 = pltpu.roll(u, 1, 0)
        v2 = pltpu.roll(cw_ref[1:2, cols] * u + cw_ref[0:1, cols] * u1, 2, 0)
        conv = (cb_ref[:, cols] + cw_ref[3:4, cols] * u[halo:halo + t]
                + cw_ref[2:3, cols] * u1[halo:halo + t] + v2[halo:halo + t])
        cbuf[:, cols] = jax.nn.silu(conv)

    def scan_chunk(zbuf, dtbuf, cbuf, out_rows, c):
        rows = slice(c * cl, (c + 1) * cl)
        dt = dtbuf[rows, :]
        acum = jnp.dot(tril_f, dt * a_row, preferred_element_type=F32,
                       precision=lax.Precision.HIGHEST)
        acum_t = acum.T
        dt_x = expand(dt)
        acum_x = expand(acum)
        xs = cbuf[rows, 0:D_SSD]
        bm = cbuf[rows, D_SSD:D_SSD + D_BC].astype(BF16)
        cm = cbuf[rows, D_SSD + D_BC:D_CONV].astype(BF16)
        bmg = [bm[:, q * SSD_STATE:(q + 1) * SSD_STATE] for q in range(SSD_GROUPS)]
        cmg = [cm[:, q * SSD_STATE:(q + 1) * SSD_STATE] for q in range(SSD_GROUPS)]
        xdt = xs * dt_x
        last = acum_x[cl - 1:cl, :]
        xdecay = (xdt * jnp.exp2(last - acum_x)).astype(BF16)
        xdt_b = xdt.astype(BF16)
        eacum = jnp.exp2(acum_x)
        cdecay = jnp.exp2(last)

        cbs = [lax.dot_general(cmg[q], bmg[q], (((1,), (1,)), ((), ())),
                               preferred_element_type=F32) for q in range(SSD_GROUPS)]

        def gmat(hd):
            seg = acum[:, hd:hd + 1] - acum_t[hd:hd + 1, :]
            return (cbs[hd // SSD_HEADS_PER_GROUP]
                    * jnp.exp2(jnp.where(tril, seg, NEG))).astype(BF16)

        y_tiles = []
        for p in range(N_PAIRS):
            ps = slice(p * LANES, (p + 1) * LANES)
            xp = xdt_b[:, ps]
            zero = jnp.zeros_like(xp)
            xpair = jnp.concatenate([jnp.where(lo, xp, zero), jnp.where(lo, zero, xp)], axis=0)
            gpair = jnp.concatenate([gmat(2 * p), gmat(2 * p + 1)], axis=1)
            y = jnp.dot(gpair, xpair, preferred_element_type=F32)
            groups, first = tile_groups(p)
            hp_b = hst[:, ps].astype(BF16)
            offs = [jnp.dot(cmg[q], hp_b, preferred_element_type=F32) for q in groups]
            sts = [lax.dot_general(bmg[q], xdecay[:, ps], (((0,), (0,)), ((), ())),
                                   preferred_element_type=F32) for q in groups]
            off = offs[0] if first is None else jnp.where(first, offs[0], offs[1])
            st = sts[0] if first is None else jnp.where(first, sts[0], sts[1])
            y_tiles.append(y + off * eacum[:, ps])
            hst[:, ps] = hst[:, ps] * cdecay[:, ps] + st

        y = jnp.concatenate(y_tiles, axis=1) + dskip_ref[...] * xs
        y = y * jax.nn.silu(zbuf[rows, :])
        inv = lax.rsqrt(jnp.mean(y * y, axis=-1, keepdims=True) + RMS_EPS)
        o_ref[out_rows.start + c * cl:out_rows.start + (c + 1) * cl, :] = (
            y * inv * nw_ref[...]).astype(BF16)

    @pl.when(g == 0)
    def _():
        ubuf1[t:t + halo, :] = jnp.zeros((halo, D_CONV), F32)
        for piece in sum(project_pieces(h0_ref, hbuf0, ubuf0, zbuf0, dtbuf0), []):
            piece()

    def run(tile, cur, nxt, h_next, out_rows):
        ubuf, zbuf, dtbuf, cbuf, _ = cur
        begin(tile, ubuf, nxt[0])
        first, second = project_pieces(h_next, nxt[4], nxt[0], nxt[1], nxt[2])
        first[0]()
        for k in range(D_CONV // PROJ_COLS):
            first[k + 1]()
            conv_piece(ubuf, cbuf, k)
        for c in range(t // cl):
            second[c]()
            scan_chunk(zbuf, dtbuf, cbuf, out_rows, c)

    assert D_SSD // PROJ_COLS + 1 == t // cl
    set0 = (ubuf0, zbuf0, dtbuf0, cbuf0, hbuf0)
    set1 = (ubuf1, zbuf1, dtbuf1, cbuf1, hbuf1)
    run(2 * g, set0, set1, ha_ref, slice(0, t))
    run(2 * g + 1, set1, set0, hb_ref, slice(t, 2 * t))


def _ssd(h, w, wdt, dtb, cw, cb, alog, dskip, nw, s):
    n = h.shape[0]
    steps = n // (2 * SSD_STEP)
    last = n // SSD_STEP - 1
    tile = lambda f: pl.BlockSpec((SSD_STEP, D_MODEL), lambda g: (f(g), 0))
    return pl.pallas_call(
        functools.partial(_ssd_kernel, s // SSD_STEP),
        grid=(steps,),
        in_specs=[tile(lambda g: 0), tile(lambda g: 2 * g + 1),
                  tile(lambda g: jnp.minimum(2 * g + 2, last)),
                  _resident((D_MODEL, D_SSD + D_CONV)), _resident((D_MODEL, DT_PAD)),
                  _resident((1, DT_PAD)), _resident((CONV_WIDTH, D_CONV)), _resident((1, D_CONV)),
                  _resident((1, DT_PAD)), _resident((1, D_SSD)), _resident((1, D_SSD))],
        out_specs=pl.BlockSpec((2 * SSD_STEP, D_SSD), lambda g: (g, 0)),
        out_shape=jax.ShapeDtypeStruct((n, D_SSD), BF16),
        scratch_shapes=[pltpu.VMEM((SSD_STEP + SUBLANES, D_CONV), F32)] * 2
        + [pltpu.VMEM((SSD_STEP, D_SSD), F32)] * 2 + [pltpu.VMEM((SSD_STEP, DT_PAD), F32)] * 2
        + [pltpu.VMEM((SSD_STEP, D_CONV), F32)] * 2 + [pltpu.VMEM((SSD_STEP, D_MODEL), BF16)] * 2
        + [pltpu.VMEM((SSD_STATE, D_SSD), F32)],
        compiler_params=pltpu.CompilerParams(
            dimension_semantics=("arbitrary",), vmem_limit_bytes=VMEM_LIMIT),
        name="ssd",
    )(h, h, h, w, wdt, dtb, cw, cb, alog, dskip, nw)


def _out_ffn_ln_kernel(alpha, a_ref, y_ref, h_ref, wo_ref, g2_ref, b2_ref,
                       wg_ref, wu_ref, wd_ref, g3_ref, b3_ref, o_ref, act_ref):
    subs = _sub_tiles(h_ref.shape[0])
    pre = [alpha * h_ref[rows, :]
           + jnp.dot(a_ref[rows, :], wo_ref[0:D_ATTN, :], preferred_element_type=F32)
           + jnp.dot(y_ref[rows, :], wo_ref[D_ATTN:D_ATTN + D_SSD, :], preferred_element_type=F32)
           for rows in subs]
    for rows, p in zip(subs, pre):
        h2 = _layer_norm(p, g2_ref[...], b2_ref[...])
        f = _swiglu(h2.astype(BF16), wg_ref, wu_ref, wd_ref, act_ref, rows)
        o_ref[rows, :] = _layer_norm(alpha * h2 + 0.5 * f, g3_ref[...], b3_ref[...])


def _out_ffn_ln(attn, y, h, wo, g2, b2, wg, wu, wd, g3, b3, alpha):
    n = h.shape[0]
    row = lambda width: pl.BlockSpec((TM_FFN, width), lambda i: (i, 0))
    vec = _resident((1, D_MODEL))
    return pl.pallas_call(
        functools.partial(_out_ffn_ln_kernel, alpha),
        grid=(n // TM_FFN,),
        in_specs=[row(D_ATTN), row(D_SSD), row(D_MODEL), _resident((D_ATTN + D_SSD, D_MODEL)),
                  vec, vec, _resident((D_MODEL, D_FF)), _resident((D_MODEL, D_FF)),
                  _resident((D_FF, D_MODEL)), vec, vec],
        out_specs=row(D_MODEL),
        out_shape=jax.ShapeDtypeStruct((n, D_MODEL), F32),
        scratch_shapes=[pltpu.VMEM((TM_FFN, D_FF), BF16)],
        compiler_params=pltpu.CompilerParams(
            dimension_semantics=("parallel",), vmem_limit_bytes=VMEM_LIMIT),
        name="out_ffn_ln",
    )(attn, y, h, wo, g2, b2, wg, wu, wd, g3, b3)


def kernel(x, positions, ln1_g, ln1_b, ffn1_gate, ffn1_up, ffn1_down, w_in, conv_w, conv_b, dt_bias, a_log, d_skip, attn_norm_w, ssd_norm_w, w_out, ln2_g, ln2_b, ffn2_gate, ffn2_up, ffn2_down, ln3_g, ln3_b):
    b, s, d = x.shape
    depth = w_in.shape[0]
    assert d == D_MODEL and s % TM == 0 and s % (DILATIONS[-1] * ATTN_BLOCK) == 0
    assert (b * s) % TM_FFN == 0 and (b * s) % (2 * SSD_STEP) == 0 and s % SSD_STEP == 0
    n = b * s
    alpha = (2.0 * depth) ** 0.25
    row = lambda v: v.reshape(1, -1).astype(F32)
    pad_lanes = lambda v: jnp.pad(row(v), ((0, 0), (0, LANES - v.shape[-1])))

    inv_freq = ROPE_THETA ** (-jnp.arange(0, ROPE_DIM, 2, dtype=F32) / ROPE_DIM)
    freq = inv_freq.reshape(ROPE_DIM // 2, 1)
    pos = positions.reshape(n // TM, 1, TM)
    h = x.reshape(n, d)
    for l in range(depth):
        h = _ffn_ln(h, ffn1_gate[l].astype(BF16), ffn1_up[l].astype(BF16), ffn1_down[l].astype(BF16),
                    row(ln1_g[l]), row(ln1_b[l]), alpha)
        wb = w_in[l].astype(BF16)
        wdt = jnp.pad(wb[:, OFF_DT:], ((0, 0), (0, DT_PAD - N_SSD_HEADS)))
        qkv = _in_proj(h, pos, freq, wb[:, :OFF_Z], b, s)
        attn = _attention(qkv, row(attn_norm_w[l]), b, s).reshape(n, D_ATTN)
        y = _ssd(h, wb[:, OFF_Z:OFF_DT], wdt, pad_lanes(dt_bias[l]), conv_w[l].astype(F32),
                 row(conv_b[l]), pad_lanes(a_log[l]), row(jnp.repeat(d_skip[l], SSD_HEAD_DIM)),
                 row(ssd_norm_w[l]), s)
        h = _out_ffn_ln(attn, y, h, w_out[l].astype(BF16), row(ln2_g[l]), row(ln2_b[l]),
                        ffn2_gate[l].astype(BF16), ffn2_up[l].astype(BF16), ffn2_down[l].astype(BF16),
                        row(ln3_g[l]), row(ln3_b[l]), alpha)
    return h.reshape(b, s, d)
```

```python
import functools
import math

import jax
import jax.numpy as jnp
from jax import lax
from jax.experimental import pallas as pl
from jax.experimental.pallas import tpu as pltpu

F32 = jnp.float32
BF16 = jnp.bfloat16

D_MODEL = 1024
HEAD_DIM = 64
N_ATTN_HEADS = 12
D_ATTN = N_ATTN_HEADS * HEAD_DIM
DILATED_BRANCHES = ((128, 1), (512, 4), (2048, 16))
ATTN_BLOCK = 128
ROPE_THETA = 500000.0
ROPE_DIM = HEAD_DIM // 4
N_SSD_HEADS = 12
SSD_HEAD_DIM = 64
D_SSD = N_SSD_HEADS * SSD_HEAD_DIM
SSD_GROUPS = 4
SSD_HEADS_PER_GROUP = N_SSD_HEADS // SSD_GROUPS
SSD_STATE = 128
CONV_WIDTH = 4
SSD_CHUNK = 128
D_BC = SSD_GROUPS * SSD_STATE
D_CONV = D_SSD + 2 * D_BC
D_FF = 2816
LN_EPS = 1e-5
RMS_EPS = 1e-6

LANES = 128
SUBLANES = 8
N_PAIRS = D_ATTN // LANES
DT_PAD = LANES
OFF_Q, OFF_K, OFF_V = 0, D_ATTN, 2 * D_ATTN
OFF_Z = 3 * D_ATTN
OFF_XBC = OFF_Z + D_SSD
OFF_DT = OFF_XBC + D_CONV
D_IN = OFF_DT + N_SSD_HEADS
NEG = -1e30
LOG2E = math.log2(math.e)

TM = 512
TM_FFN = 1024
FF_CHUNK = 256
SUB_ROWS = 256
SSD_STEP = 512
PROJ_COLS = 256
MERGE_UNROLL = 4
VMEM_LIMIT = 56 * 1024 * 1024

assert all(w // d == ATTN_BLOCK for w, d in DILATED_BRANCHES)
DILATIONS = tuple(d for _, d in DILATED_BRANCHES)
assert DILATIONS == (1, 4, 16)
STRIDE = DILATIONS[1]


def _resident(shape):
    return pl.BlockSpec(shape, lambda *_: (0,) * len(shape), pipeline_mode=pl.Buffered(1))


def _layer_norm(t, g, b):
    mu = jnp.mean(t, axis=-1, keepdims=True)
    c = t - mu
    var = jnp.mean(c * c, axis=-1, keepdims=True)
    return c * lax.rsqrt(var + LN_EPS) * g + b


def _swiglu(xb, wg_ref, wu_ref, wd_ref, act_ref, rows):
    for c in range(D_FF // FF_CHUNK):
        sl = slice(c * FF_CHUNK, (c + 1) * FF_CHUNK)
        g = jnp.dot(xb, wg_ref[:, sl], preferred_element_type=F32)
        u = jnp.dot(xb, wu_ref[:, sl], preferred_element_type=F32)
        act_ref[rows, sl] = (jax.nn.silu(g) * u).astype(BF16)
    return jnp.dot(act_ref[rows, :], wd_ref[...], preferred_element_type=F32)


def _sub_tiles(tm):
    return [slice(r, r + SUB_ROWS) for r in range(0, tm, SUB_ROWS)]


def _ffn_ln_kernel(alpha, x_ref, wg_ref, wu_ref, wd_ref, g_ref, b_ref, o_ref, act_ref):
    for rows in _sub_tiles(x_ref.shape[0]):
        x = x_ref[rows, :]
        f = _swiglu(x.astype(BF16), wg_ref, wu_ref, wd_ref, act_ref, rows)
        o_ref[rows, :] = _layer_norm(alpha * x + 0.5 * f, g_ref[...], b_ref[...])


def _ffn_ln(x, wg, wu, wd, g, b, alpha):
    n = x.shape[0]
    row = pl.BlockSpec((TM_FFN, D_MODEL), lambda i: (i, 0))
    return pl.pallas_call(
        functools.partial(_ffn_ln_kernel, alpha),
        grid=(n // TM_FFN,),
        in_specs=[row, _resident((D_MODEL, D_FF)), _resident((D_MODEL, D_FF)),
                  _resident((D_FF, D_MODEL)), _resident((1, D_MODEL)), _resident((1, D_MODEL))],
        out_specs=row,
        out_shape=jax.ShapeDtypeStruct((n, D_MODEL), F32),
        scratch_shapes=[pltpu.VMEM((TM_FFN, D_FF), BF16)],
        compiler_params=pltpu.CompilerParams(
            dimension_semantics=("parallel",), vmem_limit_bytes=VMEM_LIMIT),
        name="ffn_ln",
    )(x, wg, wu, wd, g, b)


def _in_proj_kernel(h_ref, pos_ref, freq_ref, w_ref,
                    q1, q4, q16, k1, k4, k16, v1, v4, v16, nat_ref, s4_ref):
    tm = h_ref.shape[0]
    hb = h_ref[...].astype(BF16)
    half = ROPE_DIM // 2
    q4n = tm // STRIDE
    q16n = q4n // STRIDE

    ang = freq_ref[...] * pos_ref[...].astype(F32)
    c8, s8 = jnp.cos(ang), jnp.sin(ang)
    z8 = jnp.zeros_like(c8)
    rest = HEAD_DIM - ROPE_DIM
    heads = LANES // HEAD_DIM
    cos_t = jnp.concatenate([c8, c8, jnp.ones((rest, tm), F32)] * heads, axis=0).T
    sin_lo = jnp.concatenate([s8, z8, jnp.zeros((rest, tm), F32)] * heads, axis=0).T
    sin_hi = jnp.concatenate([z8, s8, jnp.zeros((rest, tm), F32)] * heads, axis=0).T

    def project(slot, off, scale, rotate, r1):
        t = jnp.dot(hb, w_ref[:, off:off + D_ATTN], preferred_element_type=F32)
        for p in range(N_PAIRS):
            tp = t[:, p * LANES:(p + 1) * LANES]
            if rotate:
                tp = (tp * cos_t - pltpu.roll(tp, LANES - half, 1) * sin_lo
                      + pltpu.roll(tp, half, 1) * sin_hi)
            if scale != 1.0:
                tp = tp * scale
            nat_ref[slot * N_PAIRS + p] = tp
            r1[p] = tp.astype(BF16)

    def reorder(slot, r4, r16):
        for p in range(N_PAIRS):
            i = slot * N_PAIRS + p
            for r in range(STRIDE):
                t4 = nat_ref[i, pl.ds(r, q4n, stride=STRIDE), :]
                s4_ref[i, r * q4n:(r + 1) * q4n, :] = t4
                r4[p, :, r * LANES:(r + 1) * LANES] = t4.astype(BF16)
            for r in range(STRIDE):
                for a in range(STRIDE):
                    r16c = a * STRIDE + r
                    r16[p, :, r16c * LANES:(r16c + 1) * LANES] = s4_ref[
                        i, pl.ds(r * q4n + a, q16n, stride=STRIDE), :].astype(BF16)

    project(0, OFF_Q, LOG2E * HEAD_DIM ** -0.5, True, q1)
    project(1, OFF_K, 1.0, True, k1)
    reorder(0, q4, q16)
    project(2, OFF_V, 1.0, False, v1)
    reorder(1, k4, k16)
    reorder(2, v4, v16)


def _in_proj(h, pos, freq, w, b, s):
    nt = s // TM
    row = lambda width: pl.BlockSpec((TM, width), lambda i, j: (i * nt + j, 0))
    qkv_shapes, qkv_specs = [], []
    for _ in range(3):
        for d in DILATIONS:
            qkv_shapes.append(jax.ShapeDtypeStruct((b, N_PAIRS, s // d, d * LANES), BF16))
            qkv_specs.append(pl.BlockSpec((None, N_PAIRS, TM // d, d * LANES),
                                          lambda i, j: (i, 0, j, 0)))
    return pl.pallas_call(
        _in_proj_kernel,
        grid=(b, nt),
        in_specs=[row(D_MODEL), pl.BlockSpec((None, 1, TM), lambda i, j: (i * nt + j, 0, 0)),
                  _resident((ROPE_DIM // 2, 1)), _resident((D_MODEL, D_IN))],
        out_specs=qkv_specs,
        out_shape=qkv_shapes,
        scratch_shapes=[pltpu.VMEM((3 * N_PAIRS, TM, LANES), F32),
                        pltpu.VMEM((3 * N_PAIRS, TM, LANES), F32)],
        compiler_params=pltpu.CompilerParams(
            dimension_semantics=("parallel", "parallel"), vmem_limit_bytes=VMEM_LIMIT),
        name="in_proj",
    )(h, pos, freq, w)


def _attn_kernel(q1, k1, v1, q4, k4, v4, q16, k16, v16, nw_ref, o_ref,
                 va1, vb1, va4, vb4, va16, vb16, bias_ref, res1, res4, res16, acc):
    hp = pl.program_id(1)
    s = q1.shape[0]
    blk = ATTN_BLOCK
    sq = s // STRIDE
    lo = lax.broadcasted_iota(jnp.int32, (blk, LANES), 1) < HEAD_DIM

    wrow = lax.broadcasted_iota(jnp.int32, (2 * blk, 2 * blk), 0) % blk
    wcol = lax.broadcasted_iota(jnp.int32, (2 * blk, 2 * blk), 1)
    keep = ((wcol < blk) & (wcol >= wrow)) | ((wcol >= blk) & (wcol - blk <= wrow))
    bias_ref[...] = jnp.where(keep, 0.0, NEG)

    for v, va, vb in ((v1, va1, vb1), (v4, va4, vb4), (v16, va16, vb16)):
        first = lax.broadcasted_iota(jnp.int32, v.shape, 1) % LANES < HEAD_DIM
        vv = v[...]
        one = jnp.ones_like(vv)
        va[...] = jnp.where(first, vv, one)
        vb[...] = jnp.where(first, one, vv)

    def block(q, kw, vaw, vbw, bias):
        zero = jnp.zeros_like(q)
        qs = jnp.concatenate([jnp.where(lo, q, zero), jnp.where(lo, zero, q)], axis=0)
        sc = lax.dot_general(qs, kw, (((1,), (1,)), ((), ())), preferred_element_type=F32) + bias
        m = jnp.max(sc, axis=-1, keepdims=True)
        p = jnp.exp2(sc - m).astype(BF16)
        ra = jnp.dot(p[:blk], vaw, preferred_element_type=F32)
        rb = jnp.dot(p[blk:], vbw, preferred_element_type=F32)
        oc = jnp.where(lo, ra, rb)
        lx = jnp.where(lo, rb, ra)
        mc = jnp.where(lo, jnp.broadcast_to(m[:blk], (blk, LANES)),
                       jnp.broadcast_to(m[blk:], (blk, LANES)))
        return oc, mc, lx

    def run_branch(d, qr, kr, var, vbr, put):
        nblk = s // d // blk
        for r in range(d):
            ls = slice(r * LANES, (r + 1) * LANES)
            for i in range(nblk):
                k0 = max(i - 1, 0) * blk
                bias = bias_ref[...] if i > 0 else bias_ref[:, blk:]
                put(r, i, block(qr[i * blk:(i + 1) * blk, ls], kr[k0:(i + 1) * blk, ls],
                                var[k0:(i + 1) * blk, ls], vbr[k0:(i + 1) * blk, ls], bias))

    def put1(r, i, vals):
        for j, val in enumerate(vals):
            res1[j, i * blk:(i + 1) * blk, :] = val

    def put4(r, i, vals):
        for j, val in enumerate(vals):
            res4[j, r * sq + i * blk:r * sq + (i + 1) * blk, :] = val

    def put16(r, i, vals):
        a, r4 = divmod(r, STRIDE)
        for j, val in enumerate(vals):
            res16[j, pl.ds(r4 * sq + a, blk, stride=STRIDE), :] = val

    run_branch(DILATIONS[0], q1, k1, va1, vb1, put1)
    run_branch(DILATIONS[1], q4, k4, va4, vb4, put4)
    run_branch(DILATIONS[2], q16, k16, va16, vb16, put16)

    nj = sq // blk

    def merge(it, carry):
        r4, jb = it // nj, it % nj
        nat = pl.ds(jb * (STRIDE * blk) + r4, blk, stride=STRIDE)
        rows = pl.ds(pl.multiple_of(r4 * sq + jb * blk, blk), blk)
        o = (res1[0, nat, :], res4[0, rows, :], res16[0, rows, :])
        m = (res1[1, nat, :], res4[1, rows, :], res16[1, rows, :])
        l = (res1[2, nat, :], res4[2, rows, :], res16[2, rows, :])
        mx = jnp.maximum(jnp.maximum(m[0], m[1]), m[2])
        e = [jnp.exp2(mi - mx) for mi in m]
        num = e[0] * o[0] + e[1] * o[1] + e[2] * o[2]
        den = sum(ei * pltpu.roll(li, HEAD_DIM, 1) for ei, li in zip(e, l))
        acc[hp, nat, :] = num / den
        return carry
    lax.fori_loop(0, STRIDE * nj, merge, 0, unroll=MERGE_UNROLL)

    @pl.when(hp == N_PAIRS - 1)
    def _():
        def norm(i, carry):
            rows = pl.ds(pl.multiple_of(i * blk, blk), blk)
            sq_sum = acc[0, rows, :] * acc[0, rows, :]
            for p in range(1, N_PAIRS):
                sq_sum = sq_sum + acc[p, rows, :] * acc[p, rows, :]
            inv = lax.rsqrt(jnp.sum(sq_sum, axis=-1, keepdims=True) / D_ATTN + RMS_EPS)
            for p in range(N_PAIRS):
                ps = slice(p * LANES, (p + 1) * LANES)
                o_ref[rows, ps] = (acc[p, rows, :] * inv * nw_ref[:, ps]).astype(BF16)
            return carry
        lax.fori_loop(0, s // blk, norm, 0, unroll=2)


def _attention(qkv, nw, b, s):
    in_specs = []
    for _ in range(3):
        for d in DILATIONS:
            in_specs.append(pl.BlockSpec((None, None, s // d, d * LANES), lambda i, p: (i, p, 0, 0)))
    ops = [qkv[t * 3 + g] for g in range(3) for t in range(3)]
    specs = [in_specs[t * 3 + g] for g in range(3) for t in range(3)]
    vscratch = []
    for d in DILATIONS:
        vscratch += [pltpu.VMEM((s // d, d * LANES), BF16)] * 2
    return pl.pallas_call(
        _attn_kernel,
        grid=(b, N_PAIRS),
        in_specs=specs + [pl.BlockSpec((1, D_ATTN), lambda i, p: (0, 0))],
        out_specs=pl.BlockSpec((None, s, D_ATTN), lambda i, p: (i, 0, 0)),
        out_shape=jax.ShapeDtypeStruct((b, s, D_ATTN), BF16),
        scratch_shapes=vscratch + [
            pltpu.VMEM((2 * ATTN_BLOCK, 2 * ATTN_BLOCK), F32),
            pltpu.VMEM((3, s, LANES), F32), pltpu.VMEM((3, s, LANES), F32),
            pltpu.VMEM((3, s, LANES), F32), pltpu.VMEM((N_PAIRS, s, LANES), F32)],
        compiler_params=pltpu.CompilerParams(
            dimension_semantics=("parallel", "arbitrary"), vmem_limit_bytes=VMEM_LIMIT),
        name="dilated_attention",
    )(*ops, nw)


def _ssd_kernel(nt, h0_ref, ha_ref, hb_ref, w_ref, wdt_ref, dtb_ref, cw_ref, cb_ref, alog_ref,
                dskip_ref, nw_ref, o_ref, ubuf0, ubuf1, zbuf0, zbuf1, dtbuf0, dtbuf1, cbuf0, cbuf1,
                hbuf0, hbuf1, hst):
    g = pl.program_id(0)
    t = SSD_STEP
    cl = SSD_CHUNK
    halo = SUBLANES
    assert CONV_WIDTH == 4

    lo = lax.broadcasted_iota(jnp.int32, (cl, LANES), 1) < SSD_HEAD_DIM
    ri = lax.broadcasted_iota(jnp.int32, (cl, cl), 0)
    ci = lax.broadcasted_iota(jnp.int32, (cl, cl), 1)
    tril = ri >= ci
    tril_f = tril.astype(F32)
    group_w = SSD_HEADS_PER_GROUP * SSD_HEAD_DIM
    a_row = -jnp.exp(alog_ref[...]) * LOG2E

    def project_pieces(h_ref, hbuf, ubuf, zbuf, dtbuf):
        def cast():
            hbuf[...] = h_ref[...].astype(BF16)

        def xbc(k):
            cols = slice(k * PROJ_COLS, (k + 1) * PROJ_COLS)
            wcols = slice(OFF_XBC + k * PROJ_COLS, OFF_XBC + (k + 1) * PROJ_COLS)
            ubuf[halo:halo + t, cols] = jnp.dot(hbuf[...], w_ref[:, wcols],
                                                preferred_element_type=F32)

        def z(k):
            cols = slice(k * PROJ_COLS, (k + 1) * PROJ_COLS)
            wcols = slice(OFF_Z + k * PROJ_COLS, OFF_Z + (k + 1) * PROJ_COLS)
            zbuf[:, cols] = jnp.dot(hbuf[...], w_ref[:, wcols], preferred_element_type=F32)

        def dt():
            d = jnp.dot(hbuf[...], wdt_ref[...], preferred_element_type=F32)
            dtbuf[...] = jax.nn.softplus(d + dtb_ref[...])

        return ([cast] + [functools.partial(xbc, k) for k in range(D_CONV // PROJ_COLS)],
                [functools.partial(z, k) for k in range(D_SSD // PROJ_COLS)] + [dt])

    def expand(v):
        tiles = []
        for p in range(N_PAIRS):
            a = jnp.broadcast_to(v[:, 2 * p:2 * p + 1], (cl, LANES))
            b = jnp.broadcast_to(v[:, 2 * p + 1:2 * p + 2], (cl, LANES))
            tiles.append(jnp.where(lo, a, b))
        return jnp.concatenate(tiles, axis=1)

    def tile_groups(p):
        g0, g1 = (p * LANES) // group_w, ((p + 1) * LANES - 1) // group_w
        if g0 == g1:
            return (g0,), None
        first = lax.broadcasted_iota(jnp.int32, (cl, LANES), 1) < g1 * group_w - p * LANES
        return (g0, g1), first

    def begin(tile, ubuf, prev_ubuf):
        ubuf[0:halo, :] = prev_ubuf[t:t + halo, :]

        @pl.when(tile % nt == 0)
        def _():
            ubuf[0:halo, :] = jnp.zeros((halo, D_CONV), F32)
            hst[...] = jnp.zeros_like(hst)

    def conv_piece(ubuf, cbuf, k):
        cols = slice(k * PROJ_COLS, (k + 1) * PROJ_COLS)
        u = ubuf[:, cols]
        u1 = pltpu.roll(u, 1, 0)
        v2 = pltpu.roll(cw_ref[1:2, cols] * u + cw_ref[0:1, cols] * u1, 2, 0)
        conv = (cb_ref[:, cols] + cw_ref[3:4, cols] * u[halo:halo + t]
                + cw_ref[2:3, cols] * u1[halo:halo + t] + v2[halo:halo + t])
        cbuf[:, cols] = jax.nn.silu(conv)

    def scan_chunk(zbuf, dtbuf, cbuf, out_rows, c):
        rows = slice(c * cl, (c + 1) * cl)
        dt = dtbuf[rows, :]
        acum = jnp.dot(tril_f, dt * a_row, preferred_element_type=F32,
                       precision=lax.Precision.HIGHEST)
        acum_t = acum.T
        dt_x = expand(dt)
        acum_x = expand(acum)
        xs = cbuf[rows, 0:D_SSD]
        bm = cbuf[rows, D_SSD:D_SSD + D_BC].astype(BF16)
        cm = cbuf[rows, D_SSD + D_BC:D_CONV].astype(BF16)
        bmg = [bm[:, q * SSD_STATE:(q + 1) * SSD_STATE] for q in range(SSD_GROUPS)]
        cmg = [cm[:, q * SSD_STATE:(q + 1) * SSD_STATE] for q in range(SSD_GROUPS)]
        xdt = xs * dt_x
        last = acum_x[cl - 1:cl, :]
        xdecay = (xdt * jnp.exp2(last - acum_x)).astype(BF16)
        xdt_b = xdt.astype(BF16)
        eacum = jnp.exp2(acum_x)
        cdecay = jnp.exp2(last)

        cbs = [lax.dot_general(cmg[q], bmg[q], (((1,), (1,)), ((), ())),
                               preferred_element_type=F32) for q in range(SSD_GROUPS)]

        def gmat(hd):
            seg = acum[:, hd:hd + 1] - acum_t[hd:hd + 1, :]
            return (cbs[hd // SSD_HEADS_PER_GROUP]
                    * jnp.exp2(jnp.where(tril, seg, NEG))).astype(BF16)

        y_tiles = []
        for p in range(N_PAIRS):
            ps = slice(p * LANES, (p + 1) * LANES)
            xp = xdt_b[:, ps]
            zero = jnp.zeros_like(xp)
            xpair = jnp.concatenate([jnp.where(lo, xp, zero), jnp.where(lo, zero, xp)], axis=0)
            gpair = jnp.concatenate([gmat(2 * p), gmat(2 * p + 1)], axis=1)
            y = jnp.dot(gpair, xpair, preferred_element_type=F32)
            groups, first = tile_groups(p)
            hp_b = hst[:, ps].astype(BF16)
            offs = [jnp.dot(cmg[q], hp_b, preferred_element_type=F32) for q in groups]
            sts = [lax.dot_general(bmg[q], xdecay[:, ps], (((0,), (0,)), ((), ())),
                                   preferred_element_type=F32) for q in groups]
            off = offs[0] if first is None else jnp.where(first, offs[0], offs[1])
            st = sts[0] if first is None else jnp.where(first, sts[0], sts[1])
            y_tiles.append(y + off * eacum[:, ps])
            hst[:, ps] = hst[:, ps] * cdecay[:, ps] + st

        y = jnp.concatenate(y_tiles, axis=1) + dskip_ref[...] * xs
        y = y * jax.nn.silu(zbuf[rows, :])
        inv = lax.rsqrt(jnp.mean(y * y, axis=-1, keepdims=True) + RMS_EPS)
        o_ref[out_rows.start + c * cl:out_rows.start + (c + 1) * cl, :] = (
            y * inv * nw_ref[...]).astype(BF16)

    @pl.when(g == 0)
    def _():
        ubuf1[t:t + halo, :] = jnp.zeros((halo, D_CONV), F32)
        for piece in sum(project_pieces(h0_ref, hbuf0, ubuf0, zbuf0, dtbuf0), []):
            piece()

    def run(tile, cur, nxt, h_next, out_rows):
        ubuf, zbuf, dtbuf, cbuf, _ = cur
        begin(tile, ubuf, nxt[0])
        first, second = project_pieces(h_next, nxt[4], nxt[0], nxt[1], nxt[2])
        first[0]()
        for k in range(D_CONV // PROJ_COLS):
            first[k + 1]()
            conv_piece(ubuf, cbuf, k)
        for c in range(t // cl):
            second[c]()
            scan_chunk(zbuf, dtbuf, cbuf, out_rows, c)

    assert D_SSD // PROJ_COLS + 1 == t // cl
    set0 = (ubuf0, zbuf0, dtbuf0, cbuf0, hbuf0)
    set1 = (ubuf1, zbuf1, dtbuf1, cbuf1, hbuf1)
    run(2 * g, set0, set1, ha_ref, slice(0, t))
    run(2 * g + 1, set1, set0, hb_ref, slice(t, 2 * t))


def _ssd(h, w, wdt, dtb, cw, cb, alog, dskip, nw, s):
    n = h.shape[0]
    steps = n // (2 * SSD_STEP)
    last = n // SSD_STEP - 1
    tile = lambda f: pl.BlockSpec((SSD_STEP, D_MODEL), lambda g: (f(g), 0))
    return pl.pallas_call(
        functools.partial(_ssd_kernel, s // SSD_STEP),
        grid=(steps,),
        in_specs=[tile(lambda g: 0), tile(lambda g: 2 * g + 1),
                  tile(lambda g: jnp.minimum(2 * g + 2, last)),
                  _resident((D_MODEL, D_IN)), _resident((D_MODEL, DT_PAD)),
                  _resident((1, DT_PAD)), _resident((CONV_WIDTH, D_CONV)), _resident((1, D_CONV)),
                  _resident((1, DT_PAD)), _resident((1, D_SSD)), _resident((1, D_SSD))],
        out_specs=pl.BlockSpec((2 * SSD_STEP, D_SSD), lambda g: (g, 0)),
        out_shape=jax.ShapeDtypeStruct((n, D_SSD), BF16),
        scratch_shapes=[pltpu.VMEM((SSD_STEP + SUBLANES, D_CONV), F32)] * 2
        + [pltpu.VMEM((SSD_STEP, D_SSD), F32)] * 2 + [pltpu.VMEM((SSD_STEP, DT_PAD), F32)] * 2
        + [pltpu.VMEM((SSD_STEP, D_CONV), F32)] * 2 + [pltpu.VMEM((SSD_STEP, D_MODEL), BF16)] * 2
        + [pltpu.VMEM((SSD_STATE, D_SSD), F32)],
        compiler_params=pltpu.CompilerParams(
            dimension_semantics=("arbitrary",), vmem_limit_bytes=VMEM_LIMIT),
        name="ssd",
    )(h, h, h, w, wdt, dtb, cw, cb, alog, dskip, nw)


def _out_ffn_ln_kernel(alpha, a_ref, y_ref, h_ref, wo_ref, g2_ref, b2_ref,
                       wg_ref, wu_ref, wd_ref, g3_ref, b3_ref, o_ref, act_ref):
    subs = _sub_tiles(h_ref.shape[0])
    pre = [alpha * h_ref[rows, :]
           + jnp.dot(a_ref[rows, :], wo_ref[0:D_ATTN, :], preferred_element_type=F32)
           + jnp.dot(y_ref[rows, :], wo_ref[D_ATTN:D_ATTN + D_SSD, :], preferred_element_type=F32)
           for rows in subs]
    for rows, p in zip(subs, pre):
        h2 = _layer_norm(p, g2_ref[...], b2_ref[...])
        f = _swiglu(h2.astype(BF16), wg_ref, wu_ref, wd_ref, act_ref, rows)
        o_ref[rows, :] = _layer_norm(alpha * h2 + 0.5 * f, g3_ref[...], b3_ref[...])


def _out_ffn_ln(attn, y, h, wo, g2, b2, wg, wu, wd, g3, b3, alpha):
    n = h.shape[0]
    row = lambda width: pl.BlockSpec((TM_FFN, width), lambda i: (i, 0))
    vec = _resident((1, D_MODEL))
    return pl.pallas_call(
        functools.partial(_out_ffn_ln_kernel, alpha),
        grid=(n // TM_FFN,),
        in_specs=[row(D_ATTN), row(D_SSD), row(D_MODEL), _resident((D_ATTN + D_SSD, D_MODEL)),
                  vec, vec, _resident((D_MODEL, D_FF)), _resident((D_MODEL, D_FF)),
                  _resident((D_FF, D_MODEL)), vec, vec],
        out_specs=row(D_MODEL),
        out_shape=jax.ShapeDtypeStruct((n, D_MODEL), F32),
        scratch_shapes=[pltpu.VMEM((TM_FFN, D_FF), BF16)],
        compiler_params=pltpu.CompilerParams(
            dimension_semantics=("parallel",), vmem_limit_bytes=VMEM_LIMIT),
        name="out_ffn_ln",
    )(attn, y, h, wo, g2, b2, wg, wu, wd, g3, b3)


def kernel(x, positions, ln1_g, ln1_b, ffn1_gate, ffn1_up, ffn1_down, w_in, conv_w, conv_b, dt_bias, a_log, d_skip, attn_norm_w, ssd_norm_w, w_out, ln2_g, ln2_b, ffn2_gate, ffn2_up, ffn2_down, ln3_g, ln3_b):
    b, s, d = x.shape
    depth = w_in.shape[0]
    assert d == D_MODEL and s % TM == 0 and s % (DILATIONS[-1] * ATTN_BLOCK) == 0
    assert (b * s) % TM_FFN == 0 and (b * s) % (2 * SSD_STEP) == 0 and s % SSD_STEP == 0
    n = b * s
    alpha = (2.0 * depth) ** 0.25
    row = lambda v: v.reshape(1, -1).astype(F32)
    pad_lanes = lambda v: jnp.pad(row(v), ((0, 0), (0, LANES - v.shape[-1])))

    inv_freq = ROPE_THETA ** (-jnp.arange(0, ROPE_DIM, 2, dtype=F32) / ROPE_DIM)
    freq = inv_freq.reshape(ROPE_DIM // 2, 1)
    pos = positions.reshape(n // TM, 1, TM)
    h = x.reshape(n, d)
    for l in range(depth):
        h = _ffn_ln(h, ffn1_gate[l].astype(BF16), ffn1_up[l].astype(BF16), ffn1_down[l].astype(BF16),
                    row(ln1_g[l]), row(ln1_b[l]), alpha)
        wb = w_in[l].astype(BF16)
        wdt = jnp.pad(wb[:, OFF_DT:], ((0, 0), (0, DT_PAD - N_SSD_HEADS)))
        qkv = _in_proj(h, pos, freq, wb, b, s)
        attn = _attention(qkv, row(attn_norm_w[l]), b, s).reshape(n, D_ATTN)
        y = _ssd(h, wb, wdt, pad_lanes(dt_bias[l]), conv_w[l].astype(F32),
                 row(conv_b[l]), pad_lanes(a_log[l]), row(jnp.repeat(d_skip[l], SSD_HEAD_DIM)),
                 row(ssd_norm_w[l]), s)
        h = _out_ffn_ln(attn, y, h, w_out[l].astype(BF16), row(ln2_g[l]), row(ln2_b[l]),
                        ffn2_gate[l].astype(BF16), ffn2_up[l].astype(BF16), ffn2_down[l].astype(BF16),
                        row(ln3_g[l]), row(ln3_b[l]), alpha)
    return h.reshape(b, s, d)
```
